```python
import math
import jax
import jax.numpy as jnp
from jax import lax
import numpy as np

D_MODEL = 1024
BATCH = 16
SEQ = 2048
DEPTH = 4

GRID_W = 64
CTX_LEN = 256

HEAD_DIM = 64
ATT_Q_HEADS = 8
ATT_KV_HEADS = 2
ATT_GROUP = ATT_Q_HEADS // ATT_KV_HEADS
ATT_WIDTH = ATT_Q_HEADS * HEAD_DIM
KV_WIDTH = ATT_KV_HEADS * HEAD_DIM
WINDOW = 128
BLOCK = 128
ROPE_BASE = 10000.0
ROPE_PAIRS = HEAD_DIM // 4

CONV_WIDTH = 512
CONV_TAPS = 31

DN_HEADS = 4
DN_HEAD_DIM = 128
DN_WIDTH = DN_HEADS * DN_HEAD_DIM
SHORT_TAPS = 3
CHUNK = 64

BRANCH_WIDTH = 512
N_BRANCH = 3
DEEPNORM_ALPHA = (2 * DEPTH) ** 0.25
DEEPNORM_BETA = (8 * DEPTH) ** -0.25
LN_EPS = 1e-5
RMS_EPS = 1e-6
NEG_INF = -1e30

COL_NAMES = ('a_q', 'a_k', 'a_v', 'a_z', 'b_glu', 'b_z', 'c_qkv', 'c_z', 'c_a', 'c_b', 'gate')
COL_SIZES = (ATT_WIDTH, KV_WIDTH, KV_WIDTH, ATT_WIDTH, 2 * CONV_WIDTH, CONV_WIDTH,
             3 * DN_WIDTH, DN_WIDTH, 2 * DN_HEADS, 2 * DN_HEADS, N_BRANCH * D_MODEL)
IN_WIDTH = sum(COL_SIZES)

kernel_name = 'hybrid_diffusion_trunk'


def _layer_norm(t):
    tf = t.astype(jnp.float32)
    mu = jnp.mean(tf, -1, keepdims=True)
    var = jnp.mean(jnp.square(tf - mu), -1, keepdims=True)
    return ((tf - mu) * lax.rsqrt(var + LN_EPS)).astype(t.dtype)


def _split_cols(p):
    idx = np.cumsum(COL_SIZES)[:-1].tolist()
    return dict(zip(COL_NAMES, jnp.split(p, idx, axis=-1)))


def _axial_rope_tables(n):
    rows = n // GRID_W
    row = jnp.repeat(jnp.arange(rows, dtype=jnp.float32), GRID_W)
    col = jnp.tile(jnp.arange(GRID_W, dtype=jnp.float32), rows)
    inv = ROPE_BASE ** (-jnp.arange(ROPE_PAIRS, dtype=jnp.float32) / ROPE_PAIRS)
    ang = jnp.stack([row[:, None] * inv, col[:, None] * inv], axis=1)
    return jnp.cos(ang), jnp.sin(ang)


def _apply_axial_rope(t, cos, sin):
    t4 = t.reshape(*t.shape[:-1], 2, 2, ROPE_PAIRS)
    x1, x2 = t4[..., 0, :], t4[..., 1, :]
    cs = cos[None, :, None].astype(t.dtype)
    sn = sin[None, :, None].astype(t.dtype)
    out = jnp.stack([x1 * cs - x2 * sn, x2 * cs + x1 * sn], axis=-2)
    return out.reshape(t.shape)


def _band_ctx_attention(q, k, v, kc, vc, sink):
    B, n = q.shape[:2]
    nb = n // BLOCK
    L = kc.shape[1]
    scale = HEAD_DIM ** -0.5
    qb = q.reshape(B, nb, BLOCK, ATT_KV_HEADS, ATT_GROUP, HEAD_DIM)

    def band(t):
        tp = jnp.pad(t, ((0, 0), (BLOCK, BLOCK), (0, 0), (0, 0)))
        tp = tp.reshape(B, nb + 2, BLOCK, ATT_KV_HEADS, HEAD_DIM)
        return jnp.concatenate([tp[:, :-2], tp[:, 1:-1], tp[:, 2:]], axis=2)

    kb, vb = band(k), band(v)
    s_band = jnp.einsum('bnqhgd,bnkhd->bnhgqk', qb, kb).astype(jnp.float32) * scale
    qpos = jnp.arange(nb)[:, None] * BLOCK + jnp.arange(BLOCK)[None, :]
    kpos = (jnp.arange(nb)[:, None] - 1) * BLOCK + jnp.arange(3 * BLOCK)[None, :]
    kp = kpos[:, None, :]
    valid = (jnp.abs(kp - qpos[:, :, None]) <= WINDOW) & (kp >= 0) & (kp < n)
    s_band = jnp.where(valid[None, :, None, None], s_band, NEG_INF)
    s_ctx = jnp.einsum('bnqhgd,blhd->bnhgql', qb, kc).astype(jnp.float32) * scale
    s_sink = jnp.broadcast_to(sink.astype(jnp.float32).reshape(1, 1, ATT_KV_HEADS, ATT_GROUP, 1, 1),
                              s_band.shape[:-1] + (1,))
    p = jax.nn.softmax(jnp.concatenate([s_band, s_ctx, s_sink], axis=-1), axis=-1).astype(v.dtype)
    nk = 3 * BLOCK
    o = (jnp.einsum('bnhgqk,bnkhd->bnqhgd', p[..., :nk], vb)
         + jnp.einsum('bnhgql,blhd->bnqhgd', p[..., nk:nk + L], vc))
    return o.reshape(B, n, ATT_WIDTH)


def _ctx_attention(qc, kc, vc, sink):
    B, L = qc.shape[:2]
    scale = HEAD_DIM ** -0.5
    qg = qc.reshape(B, L, ATT_KV_HEADS, ATT_GROUP, HEAD_DIM)
    s = jnp.einsum('blhgd,bmhd->bhglm', qg, kc).astype(jnp.float32) * scale
    s_sink = jnp.broadcast_to(sink.astype(jnp.float32).reshape(1, ATT_KV_HEADS, ATT_GROUP, 1, 1),
                              s.shape[:-1] + (1,))
    p = jax.nn.softmax(jnp.concatenate([s, s_sink], axis=-1), axis=-1).astype(vc.dtype)
    o = jnp.einsum('bhglm,bmhd->blhgd', p[..., :L], vc)
    return o.reshape(B, L, ATT_WIDTH)


def _depthwise_conv(t, w):
    K = w.shape[0]
    return lax.conv_general_dilated(t, w[:, None, :].astype(t.dtype), window_strides=(1,),
                                    padding=[(K // 2, K // 2)],
                                    dimension_numbers=('NWC', 'WIO', 'NWC'),
                                    feature_group_count=t.shape[-1])


def _conformer_conv(glu_in, z, conv_w, conv_b, norm_g, norm_b):
    a, b = jnp.split(glu_in, 2, axis=-1)
    h = a * jax.nn.sigmoid(b)
    h = _depthwise_conv(h, conv_w) + conv_b
    h = _layer_norm(h) * norm_g + norm_b
    return jax.nn.silu(h) * jax.nn.silu(z)


def _l2norm(t):
    return t * lax.rsqrt(jnp.sum(t * t, axis=-1, keepdims=True) + RMS_EPS)


def _deltanet_qkv(qkv, conv_w):
    h = jax.nn.silu(_depthwise_conv(qkv, conv_w)).astype(jnp.float32)
    B, n = h.shape[:2]
    q, k, v = jnp.split(h, 3, axis=-1)
    shp = (B, n, DN_HEADS, DN_HEAD_DIM)
    q = _l2norm(q.reshape(shp)) * (DN_HEAD_DIM ** -0.5)
    k = _l2norm(k.reshape(shp))
    return q, k, v.reshape(shp)


def _decay_beta(a_cols, b_cols, a_log, dt_bias, d):
    sl = slice(d * DN_HEADS, (d + 1) * DN_HEADS)
    g = -jnp.exp(a_log[d].astype(jnp.float32)) * jax.nn.softplus(
        a_cols[..., sl].astype(jnp.float32) + dt_bias[d].astype(jnp.float32))
    beta = jax.nn.sigmoid(b_cols[..., sl].astype(jnp.float32))
    return g, beta


def _rev(t, d):
    return jnp.flip(t, axis=1) if d == 1 else t


def _chunk_gated_delta(q, k, v, g, beta, state, want_out):
    B, n, H, _ = k.shape
    dv = v.shape[-1]
    nc = n // CHUNK

    def to_chunks(t):
        t = t.reshape(B, nc, CHUNK, H, *t.shape[3:])
        return jnp.moveaxis(t, (1, 3), (0, 2))

    q, k, v, g, beta = [to_chunks(t) for t in (q, k, v, g, beta)]
    gc = jnp.cumsum(g, axis=-1)
    idx = jnp.arange(CHUNK)
    incl = idx[:, None] >= idx[None, :]
    strict = idx[:, None] > idx[None, :]
    dmask = jnp.exp(jnp.where(incl, gc[..., :, None] - gc[..., None, :], NEG_INF))
    kb = k * beta[..., None]
    A = jnp.where(strict, jnp.einsum('...id,...jd->...ij', kb, k) * dmask, 0.0)
    eye = jnp.eye(CHUNK, dtype=jnp.float32)
    T = lax.linalg.triangular_solve(eye + A, jnp.broadcast_to(eye, A.shape),
                                    left_side=True, lower=True, unit_diagonal=True)
    u = T @ (v * beta[..., None])
    w = T @ (kb * jnp.exp(gc)[..., None])
    gl = gc[..., -1:]
    k_dec = k * jnp.exp(gl - gc)[..., None]
    c_dec = jnp.exp(gl)[..., None]
    if want_out:
        q_dec = q * jnp.exp(gc)[..., None]
        aqk = jnp.where(incl, jnp.einsum('...id,...jd->...ij', q, k) * dmask, 0.0)

        def step(S, xs):
            u_c, w_c, kd_c, cd_c, qd_c, a_c = xs
            v_new = u_c - w_c @ S
            o = qd_c @ S + a_c @ v_new
            S = S * cd_c + jnp.swapaxes(kd_c, -1, -2) @ v_new
            return S, o

        S, o = lax.scan(step, state, (u, w, k_dec, c_dec, q_dec, aqk))
        return jnp.moveaxis(o, (0, 2), (1, 3)).reshape(B, n, H, dv), S

    def step_state(S, xs):
        u_c, w_c, kd_c, cd_c = xs
        v_new = u_c - w_c @ S
        return S * cd_c + jnp.swapaxes(kd_c, -1, -2) @ v_new, None

    S, _ = lax.scan(step_state, state, (u, w, k_dec, c_dec))
    return None, S


def _gated_rmsnorm(o, z, gain):
    o = o * lax.rsqrt(jnp.mean(o * o, axis=-1, keepdims=True) + RMS_EPS) * gain.astype(jnp.float32)
    return o.reshape(*o.shape[:2], DN_WIDTH).astype(z.dtype) * jax.nn.silu(z)


def _merge(ys, gate_logits, w_branch, w_out):
    gl = gate_logits.reshape(*gate_logits.shape[:-1], N_BRANCH, D_MODEL)
    merged = jax.nn.sigmoid(gl[..., 0, :]) * (ys[0] @ w_branch[0])
    for r in range(1, N_BRANCH):
        merged = merged + jax.nn.sigmoid(gl[..., r, :]) * (ys[r] @ w_branch[r])
    return merged @ w_out


def _layer(x, ctx, c, c_ctx, w_ada, b_ada, w_in, a_sink, b_conv_w, b_conv_b, b_norm_g, b_norm_b,
           c_conv_w, c_a_log, c_dt_bias, c_norm_g, w_branch, w_out, ln_g, ln_b, rope, need_ctx_out):
    B, n = x.shape[:2]
    L = ctx.shape[1]
    cos, sin = rope
    shift, scale, gate = jnp.split(jax.nn.silu(c) @ w_ada + b_ada, 3, axis=-1)
    shift_c, scale_c, gate_c = jnp.split(jax.nn.silu(c_ctx) @ w_ada + b_ada, 3, axis=-1)
    u = x * (1 + scale[:, None]) + shift[:, None]
    uc = ctx * (1 + scale_c) + shift_c
    P = _split_cols(u @ w_in)
    Pc = _split_cols(uc @ w_in)

    q = _apply_axial_rope(P['a_q'].reshape(B, n, ATT_Q_HEADS, HEAD_DIM), cos, sin)
    k = _apply_axial_rope(P['a_k'].reshape(B, n, ATT_KV_HEADS, HEAD_DIM), cos, sin)
    v = P['a_v'].reshape(B, n, ATT_KV_HEADS, HEAD_DIM)
    kc = Pc['a_k'].reshape(B, L, ATT_KV_HEADS, HEAD_DIM)
    vc = Pc['a_v'].reshape(B, L, ATT_KV_HEADS, HEAD_DIM)
    y_a = _band_ctx_attention(q, k, v, kc, vc, a_sink) * jax.nn.silu(P['a_z'])

    y_b = _conformer_conv(P['b_glu'], P['b_z'], b_conv_w, b_conv_b, b_norm_g, b_norm_b)

    qkv_l = _deltanet_qkv(P['c_qkv'], c_conv_w)
    qkv_c = _deltanet_qkv(Pc['c_qkv'], c_conv_w)
    state0 = jnp.zeros((B, DN_HEADS, DN_HEAD_DIM, DN_HEAD_DIM), jnp.float32)
    o_lat = None
    o_ctx = None
    for d in range(2):
        g_l, be_l = _decay_beta(P['c_a'], P['c_b'], c_a_log, c_dt_bias, d)
        g_c, be_c = _decay_beta(Pc['c_a'], Pc['c_b'], c_a_log, c_dt_bias, d)
        oc, s_ctx = _chunk_gated_delta(*[_rev(t, d) for t in (*qkv_c, g_c, be_c)], state0, need_ctx_out)
        ol, _ = _chunk_gated_delta(*[_rev(t, d) for t in (*qkv_l, g_l, be_l)], s_ctx, True)
        ol = _rev(ol, d)
        o_lat = ol if o_lat is None else o_lat + ol
        if need_ctx_out:
            oc = _rev(oc, d)
            o_ctx = oc if o_ctx is None else o_ctx + oc
    y_c = _gated_rmsnorm(o_lat, P['c_z'], c_norm_g)

    out = _merge((y_a, y_b, y_c), P['gate'], w_branch, w_out)
    x_new = _layer_norm(DEEPNORM_ALPHA * x + gate[:, None] * out) * ln_g + ln_b

    if need_ctx_out:
        qc = Pc['a_q'].reshape(B, L, ATT_Q_HEADS, HEAD_DIM)
        yc_a = _ctx_attention(qc, kc, vc, a_sink) * jax.nn.silu(Pc['a_z'])
        yc_b = _conformer_conv(Pc['b_glu'], Pc['b_z'], b_conv_w, b_conv_b, b_norm_g, b_norm_b)
        yc_c = _gated_rmsnorm(o_ctx, Pc['c_z'], c_norm_g)
        out_c = _merge((yc_a, yc_b, yc_c), Pc['gate'], w_branch, w_out)
        ctx = _layer_norm(DEEPNORM_ALPHA * ctx + gate_c * out_c) * ln_g + ln_b
    return x_new, ctx


def setup_inputs(seed: int = 0) -> dict:
    key = jax.random.key(seed)
    ks = jax.random.split(key, 24)

    def nrm(k, shape, s):
        return jax.random.normal(k, shape, jnp.float32) * s

    x = nrm(ks[0], (BATCH, SEQ, D_MODEL), 1.0)
    c = nrm(ks[1], (BATCH, D_MODEL), 1.0)
    ctx = nrm(ks[2], (BATCH, CTX_LEN, D_MODEL), 1.0)
    c_ctx = nrm(ks[3], (D_MODEL,), 1.0)
    w_ada = nrm(ks[4], (DEPTH, D_MODEL, 3 * D_MODEL), D_MODEL ** -0.5)
    b_ada = nrm(ks[5], (DEPTH, 3 * D_MODEL), 0.02)
    w_in = nrm(ks[6], (DEPTH, D_MODEL, IN_WIDTH), D_MODEL ** -0.5)
    a_sink = nrm(ks[7], (DEPTH, ATT_Q_HEADS), 0.5)
    b_conv_w = nrm(ks[8], (DEPTH, CONV_TAPS, CONV_WIDTH), CONV_TAPS ** -0.5)
    b_conv_b = nrm(ks[9], (DEPTH, CONV_WIDTH), 0.02)
    b_norm_g = 1.0 + nrm(ks[10], (DEPTH, CONV_WIDTH), 0.02)
    b_norm_b = nrm(ks[11], (DEPTH, CONV_WIDTH), 0.02)
    c_conv_w = nrm(ks[12], (DEPTH, SHORT_TAPS, 3 * DN_WIDTH), SHORT_TAPS ** -0.5)
    c_a_log = jnp.log(jax.random.uniform(ks[13], (DEPTH, 2, DN_HEADS), jnp.float32, 1.0, 16.0))
    dt = jnp.exp(jax.random.uniform(ks[14], (DEPTH, 2, DN_HEADS), jnp.float32,
                                    math.log(1e-3), math.log(1e-1)))
    c_dt_bias = dt + jnp.log(-jnp.expm1(-dt))
    c_norm_g = 1.0 + nrm(ks[15], (DEPTH, DN_HEAD_DIM), 0.02)
    w_branch = nrm(ks[16], (DEPTH, N_BRANCH, BRANCH_WIDTH, D_MODEL), BRANCH_WIDTH ** -0.5 * DEEPNORM_BETA)
    w_out = nrm(ks[17], (DEPTH, D_MODEL, D_MODEL), D_MODEL ** -0.5 * DEEPNORM_BETA)
    ln_g = 1.0 + nrm(ks[18], (DEPTH, D_MODEL), 0.02)
    ln_b = nrm(ks[19], (DEPTH, D_MODEL), 0.02)
    return {'x': x, 'c': c, 'ctx': ctx, 'c_ctx': c_ctx, 'w_ada': w_ada, 'b_ada': b_ada, 'w_in': w_in,
            'a_sink': a_sink, 'b_conv_w': b_conv_w, 'b_conv_b': b_conv_b, 'b_norm_g': b_norm_g,
            'b_norm_b': b_norm_b, 'c_conv_w': c_conv_w, 'c_a_log': c_a_log, 'c_dt_bias': c_dt_bias,
            'c_norm_g': c_norm_g, 'w_branch': w_branch, 'w_out': w_out, 'ln_g': ln_g, 'ln_b': ln_b}


def reference(x, c, ctx, c_ctx, w_ada, b_ada, w_in, a_sink, b_conv_w, b_conv_b, b_norm_g, b_norm_b,
              c_conv_w, c_a_log, c_dt_bias, c_norm_g, w_branch, w_out, ln_g, ln_b):
    rope = _axial_rope_tables(x.shape[1])
    for l in range(DEPTH):
        x, ctx = _layer(x, ctx, c, c_ctx, w_ada[l], b_ada[l], w_in[l], a_sink[l], b_conv_w[l],
                        b_conv_b[l], b_norm_g[l], b_norm_b[l], c_conv_w[l], c_a_log[l], c_dt_bias[l],
                        c_norm_g[l], w_branch[l], w_out[l], ln_g[l], ln_b[l], rope, l < DEPTH - 1)
    return x
```

```python
import functools

import jax
import jax.numpy as jnp
from jax import lax
from jax.experimental import pallas as pl
from jax.experimental.pallas import tpu as pltpu

F32 = jnp.float32
BF16 = jnp.bfloat16
HIGHEST = lax.Precision.HIGHEST

LANES = 128
VMEM_LIMIT = 56 * 1024 * 1024

D_MODEL = 1024
DEPTH = 4
GRID_W = 64
HEAD_DIM = 64
ATT_Q_HEADS = 8
ATT_KV_HEADS = 2
ATT_GROUP = ATT_Q_HEADS // ATT_KV_HEADS
ATT_WIDTH = 512
BLOCK = 128
ROPE_BASE = 10000.0
ROPE_PAIRS = HEAD_DIM // 4
CONV_WIDTH = 512
CONV_TAPS = 31
CONV_HALO = 16
DN_HEADS = 4
DN_HEAD_DIM = 128
DN_WIDTH = 512
CHUNK = 64
N_BRANCH = 3
DEEPNORM_ALPHA = (2 * DEPTH) ** 0.25
LN_EPS = 1e-5
RMS_EPS = 1e-6
NEG_INF = -1e30

OFF_AQ, OFF_AZ = 0, 512
OFF_BA, OFF_BB, OFF_BZ = 1024, 1536, 2048
OFF_CZ = 2560
OFF_GATE = 3072
OFF_CQ, OFF_CK, OFF_CV = 6144, 6656, 7168
OFF_AK, OFF_AV = 7680, 7808
OFF_AB = 7936
IN_WIDTH_PAD = 8064
IN_TILE_N = 1152
ROW_TILE = 256


def _silu(t):
    return t * jax.nn.sigmoid(t)


def _params(*sem):
    return pltpu.CompilerParams(dimension_semantics=sem, vmem_limit_bytes=VMEM_LIMIT)


def _ada_kernel(cc_ref, w_ref, b_ref, o_ref):
    s = _silu(cc_ref[...])
    o_ref[0] = jnp.dot(s, w_ref[0], precision=HIGHEST, preferred_element_type=F32) + b_ref[0]


def _ada_call(cc, w_ada, b_ada):
    rows = cc.shape[0]
    return pl.pallas_call(
        _ada_kernel,
        grid=(DEPTH, 3),
        in_specs=[
            pl.BlockSpec((rows, D_MODEL), lambda l, j: (0, 0)),
            pl.BlockSpec((1, D_MODEL, D_MODEL), lambda l, j: (l, 0, j)),
            pl.BlockSpec((1, 1, D_MODEL), lambda l, j: (l, 0, j)),
        ],
        out_specs=pl.BlockSpec((1, rows, D_MODEL), lambda l, j: (l, 0, j)),
        out_shape=jax.ShapeDtypeStruct((DEPTH, rows, 3 * D_MODEL), F32),
        compiler_params=_params("arbitrary", "arbitrary"),
        name="ada_mod",
    )(cc, w_ada, b_ada.reshape(DEPTH, 1, 3 * D_MODEL))


def _inproj_kernel(x_ref, mb_ref, mc_ref, w_ref, p_ref, ab_ref, u_scr, *, tm, ctx_len, nj):
    i = pl.program_id(1)
    j = pl.program_id(2)

    @pl.when(j == 0)
    def _():
        row = lax.broadcasted_iota(jnp.int32, (tm, 1), 0) + i * tm
        is_ctx = row < ctx_len
        shift = jnp.where(is_ctx, mc_ref[0, 0:1, :], mb_ref[0, 0:1, :])
        scale = jnp.where(is_ctx, mc_ref[0, 1:2, :], mb_ref[0, 1:2, :])
        u_scr[...] = (x_ref[0] * (1.0 + scale) + shift).astype(BF16)

    p = jnp.dot(u_scr[...], w_ref[...], preferred_element_type=F32)
    p_ref[0] = p.astype(p_ref.dtype)

    @pl.when(j == nj - 1)
    def _():
        ab_ref[0] = p[:, IN_TILE_N - LANES:]


def _inproj_call(xc, mods, w, ctx_len, p_dtype):
    B, T, _ = xc.shape
    tm = T // 2
    nj = IN_WIDTH_PAD // IN_TILE_N
    ctx_row = B
    return pl.pallas_call(
        functools.partial(_inproj_kernel, tm=tm, ctx_len=ctx_len, nj=nj),
        grid=(B, T // tm, nj),
        in_specs=[
            pl.BlockSpec((1, tm, D_MODEL), lambda b, i, j: (b, i, 0)),
            pl.BlockSpec((1, 3, D_MODEL), lambda b, i, j: (b, 0, 0)),
            pl.BlockSpec((1, 3, D_MODEL), lambda b, i, j: (ctx_row, 0, 0)),
            pl.BlockSpec((D_MODEL, IN_TILE_N), lambda b, i, j: (0, j)),
        ],
        out_specs=[
            pl.BlockSpec((1, tm, IN_TILE_N), lambda b, i, j: (b, i, j)),
            pl.BlockSpec((1, tm, LANES), lambda b, i, j: (b, i, 0)),
        ],
        out_shape=[
            jax.ShapeDtypeStruct((B, T, IN_WIDTH_PAD), p_dtype),
            jax.ShapeDtypeStruct((B, T, LANES), F32),
        ],
        scratch_shapes=[pltpu.VMEM((tm, D_MODEL), BF16)],
        compiler_params=_params("arbitrary", "arbitrary", "arbitrary"),
        name="in_proj",
    )(xc, mods, mods, w)


def _rope(t, cos, sin_signed):
    lane = lax.broadcasted_iota(jnp.int32, (1, LANES), 1)
    partner = jnp.where((lane % 32) < 16, pltpu.roll(t, LANES - 16, 1), pltpu.roll(t, 16, 1))
    return t * cos + partner * sin_signed


def _attn_kernel(sink_ref, q_ref, z_ref, k_ref, v_ref, cq_ref, sq_ref, ck_ref, sk_ref, o_ref, kr_scr,
                 *, ctx_len, n_blocks):
    i = pl.program_id(1)
    ctx_blocks = ctx_len // BLOCK

    @pl.when(i == 0)
    def _():
        kr_scr[...] = _rope(k_ref[0].astype(F32), ck_ref[...], sk_ref[...]).astype(BF16)

    q = q_ref[0].astype(F32)
    cq = cq_ref[...]
    sq = sq_ref[...]
    q_pairs = [_rope(q[:, LANES * t:LANES * (t + 1)], cq, sq) for t in range(ATT_WIDTH // LANES)]

    def q_head(hq):
        off = (hq % 2) * HEAD_DIM
        return q_pairs[hq // 2][:, off:off + HEAD_DIM]

    left = pl.multiple_of(jnp.maximum(i - 1, 0) * BLOCK, BLOCK)
    mid = pl.multiple_of(i * BLOCK, BLOCK)
    right = pl.multiple_of(jnp.minimum(i + 1, n_blocks - 1) * BLOCK, BLOCK)
    kcat = jnp.concatenate([kr_scr[0:ctx_len, :], kr_scr[pl.ds(left, BLOCK), :],
                            kr_scr[pl.ds(mid, BLOCK), :], kr_scr[pl.ds(right, BLOCK), :]], axis=0)
    vcat = jnp.concatenate([v_ref[0, 0:ctx_len, :], v_ref[0, pl.ds(left, BLOCK), :],
                            v_ref[0, pl.ds(mid, BLOCK), :], v_ref[0, pl.ds(right, BLOCK), :]],
                           axis=0).astype(BF16)

    nk = ctx_len + 3 * BLOCK
    rows = ATT_GROUP * BLOCK
    col = lax.broadcasted_iota(jnp.int32, (1, nk), 1)
    is_lat = i >= ctx_blocks
    has_left = i >= ctx_blocks + 1
    has_right = jnp.logical_and(is_lat, i <= n_blocks - 2)
    full = BLOCK - 1
    hi_left = jnp.where(has_left, col - ctx_len, -1)
    hi_mid = jnp.where(is_lat, full, -1)
    lo_right = jnp.where(has_right, col - ctx_len - 2 * BLOCK, BLOCK)
    hi = jnp.where(col < ctx_len, full,
                   jnp.where(col < ctx_len + BLOCK, hi_left,
                             jnp.where(col < ctx_len + 2 * BLOCK, hi_mid, full)))
    lo = jnp.where(col < ctx_len + 2 * BLOCK, 0, lo_right)
    rq = lax.broadcasted_iota(jnp.int32, (rows, 1), 0) % BLOCK
    grp = lax.broadcasted_iota(jnp.int32, (rows, 1), 0) // BLOCK
    valid = jnp.logical_and(rq >= lo, rq <= hi)

    scale = HEAD_DIM ** -0.5
    outs = []
    for h in range(ATT_KV_HEADS):
        qh = jnp.concatenate([q_head(h * ATT_GROUP + g) for g in range(ATT_GROUP)], axis=0).astype(BF16)
        kh = kcat[:, h * HEAD_DIM:(h + 1) * HEAD_DIM]
        vh = vcat[:, h * HEAD_DIM:(h + 1) * HEAD_DIM]
        s = lax.dot_general(qh, kh, (((1,), (1,)), ((), ())), preferred_element_type=F32) * scale
        s = jnp.where(valid, s, NEG_INF)
        sk = jnp.zeros((rows, 1), F32)
        for g in range(ATT_GROUP):
            sk = jnp.where(grp == g, sink_ref[h * ATT_GROUP + g], sk)
        m = jnp.maximum(jnp.max(s, axis=1, keepdims=True), sk)
        p = jnp.exp(s - m)
        den = jnp.sum(p, axis=1, keepdims=True) + jnp.exp(sk - m)
        o = jnp.dot(p.astype(BF16), vh, preferred_element_type=F32) / den
        outs.extend(o[g * BLOCK:(g + 1) * BLOCK] for g in range(ATT_GROUP))
    att = jnp.concatenate(outs, axis=1)
    o_ref[0] = att * _silu(z_ref[0].astype(F32))


def _attn_call(p, sink, cos, sin, ctx_len):
    B, T, _ = p.shape
    nb = T // BLOCK
    return pl.pallas_call(
        functools.partial(_attn_kernel, ctx_len=ctx_len, n_blocks=nb),
        grid=(B, nb),
        in_specs=[
            pl.BlockSpec(memory_space=pltpu.SMEM),
            pl.BlockSpec((1, BLOCK, ATT_WIDTH), lambda b, i: (b, i, OFF_AQ // ATT_WIDTH)),
            pl.BlockSpec((1, BLOCK, ATT_WIDTH), lambda b, i: (b, i, OFF_AZ // ATT_WIDTH)),
            pl.BlockSpec((1, T, LANES), lambda b, i: (b, 0, OFF_AK // LANES)),
            pl.BlockSpec((1, T, LANES), lambda b, i: (b, 0, OFF_AV // LANES)),
            pl.BlockSpec((BLOCK, LANES), lambda b, i: (i, 0)),
            pl.BlockSpec((BLOCK, LANES), lambda b, i: (i, 0)),
            pl.BlockSpec((T, LANES), lambda b, i: (0, 0)),
            pl.BlockSpec((T, LANES), lambda b, i: (0, 0)),
        ],
        out_specs=pl.BlockSpec((1, BLOCK, ATT_WIDTH), lambda b, i: (b, i, 0)),
        out_shape=jax.ShapeDtypeStruct((B, T, ATT_WIDTH), F32),
        scratch_shapes=[pltpu.VMEM((T, LANES), BF16)],
        compiler_params=_params("arbitrary", "arbitrary"),
        name="attention",
    )(sink, p, p, p, p, cos, sin, cos, sin)


def _rope_tables(n, ctx_len):
    pos = jnp.arange(n)
    row = (pos // GRID_W).astype(F32)
    colp = (pos % GRID_W).astype(F32)
    inv = ROPE_BASE ** (-jnp.arange(ROPE_PAIRS, dtype=F32) / ROPE_PAIRS)
    lane = jnp.arange(LANES)
    axis = (lane % HEAD_DIM) // (2 * ROPE_PAIRS)
    pair = lane % ROPE_PAIRS
    ang = jnp.where(axis[None, :] == 0, row[:, None], colp[:, None]) * inv[pair][None, :]
    sign = jnp.where((lane % (2 * ROPE_PAIRS)) < ROPE_PAIRS, -1.0, 1.0).astype(F32)
    cos = jnp.concatenate([jnp.ones((ctx_len, LANES), F32), jnp.cos(ang)], axis=0)
    sin = jnp.concatenate([jnp.zeros((ctx_len, LANES), F32), jnp.sin(ang) * sign[None, :]], axis=0)
    return cos, sin


def _conv_kernel(a_ref, b_ref, z_ref, ap_ref, bp_ref, an_ref, bn_ref, w_ref, cb_ref, g_ref, nb_ref,
                 o_ref, h_scr, *, ctx_tiles, n_tiles):
    i = pl.program_id(1)

    def glu(a, b):
        return a.astype(F32) * jax.nn.sigmoid(b.astype(F32))

    has_prev = jnp.logical_and(i != 0, i != ctx_tiles)
    has_next = jnp.logical_and(i != ctx_tiles - 1, i != n_tiles - 1)
    h_scr[0:CONV_HALO, :] = jnp.where(has_prev, glu(ap_ref[0], bp_ref[0]), 0.0)
    h_scr[CONV_HALO:CONV_HALO + ROW_TILE, :] = glu(a_ref[0], b_ref[0])
    h_scr[CONV_HALO + ROW_TILE:, :] = jnp.where(has_next, glu(an_ref[0], bn_ref[0]), 0.0)

    base = CONV_HALO - CONV_TAPS // 2
    acc = jnp.zeros((ROW_TILE, CONV_WIDTH), F32)
    for k in range(CONV_TAPS):
        acc = acc + w_ref[k:k + 1, :] * h_scr[base + k:base + k + ROW_TILE, :]
    acc = acc + cb_ref[...]
    mu = jnp.mean(acc, axis=1, keepdims=True)
    cen = acc - mu
    var = jnp.mean(cen * cen, axis=1, keepdims=True)
    hn = cen * lax.rsqrt(var + LN_EPS) * g_ref[...] + nb_ref[...]
    o_ref[0] = _silu(hn) * _silu(z_ref[0].astype(F32))


def _conv_call(p, conv_w, conv_b, norm_g, norm_b, ctx_len):
    B, T, _ = p.shape
    nt = T // ROW_TILE
    hpt = ROW_TILE // CONV_HALO
    last_halo = T // CONV_HALO - 1
    ca, cb_, cz = OFF_BA // CONV_WIDTH, OFF_BB // CONV_WIDTH, OFF_BZ // CONV_WIDTH

    def prev_map(col):
        return lambda b, i: (b, jnp.maximum(i * hpt - 1, 0), col)

    def next_map(col):
        return lambda b, i: (b, jnp.minimum((i + 1) * hpt, last_halo), col)

    row = lambda v: v.reshape(1, CONV_WIDTH)
    vec_spec = pl.BlockSpec((1, CONV_WIDTH), lambda b, i: (0, 0))
    return pl.pallas_call(
        functools.partial(_conv_kernel, ctx_tiles=ctx_len // ROW_TILE, n_tiles=nt),
        grid=(B, nt),
        in_specs=[
            pl.BlockSpec((1, ROW_TILE, CONV_WIDTH), lambda b, i: (b, i, ca)),
            pl.BlockSpec((1, ROW_TILE, CONV_WIDTH), lambda b, i: (b, i, cb_)),
            pl.BlockSpec((1, ROW_TILE, CONV_WIDTH), lambda b, i: (b, i, cz)),
            pl.BlockSpec((1, CONV_HALO, CONV_WIDTH), prev_map(ca)),
            pl.BlockSpec((1, CONV_HALO, CONV_WIDTH), prev_map(cb_)),
            pl.BlockSpec((1, CONV_HALO, CONV_WIDTH), next_map(ca)),
            pl.BlockSpec((1, CONV_HALO, CONV_WIDTH), next_map(cb_)),
            pl.BlockSpec((CONV_TAPS, CONV_WIDTH), lambda b, i: (0, 0)),
            vec_spec, vec_spec, vec_spec,
        ],
        out_specs=pl.BlockSpec((1, ROW_TILE, CONV_WIDTH), lambda b, i: (b, i, 0)),
        out_shape=jax.ShapeDtypeStruct((B, T, CONV_WIDTH), F32),
        scratch_shapes=[pltpu.VMEM((ROW_TILE + 2 * CONV_HALO, CONV_WIDTH), F32)],
        compiler_params=_params("arbitrary", "arbitrary"),
        name="conformer_conv",
    )(p, p, p, p, p, p, p, conv_w, row(conv_b), row(norm_g), row(norm_b))


def _split_bf16(t):
    hi = t.astype(BF16)
    return hi, (t - hi.astype(F32)).astype(BF16)


def _unit_triangular_inverse(a, eye, pair_masks):
    inv = eye - jnp.where(pair_masks[0], a, 0.0)
    for m in pair_masks[1:]:
        inv_b = inv.astype(BF16)
        t = jnp.dot(jnp.where(m, a, 0.0).astype(BF16), inv_b, preferred_element_type=F32)
        inv = inv - jnp.dot(inv_b, t.astype(BF16), preferred_element_type=F32)
    a_hi, a_lo = _split_bf16(a)
    i_hi, i_lo = _split_bf16(inv)
    prod = (jnp.dot(a_hi, i_hi, preferred_element_type=F32) + jnp.dot(a_lo, i_hi, preferred_element_type=F32)
            + jnp.dot(a_hi, i_lo, preferred_element_type=F32))
    resid = eye - inv - prod
    return inv + jnp.dot(i_hi, resid.astype(BF16), preferred_element_type=F32)


def _dn_kernel(q_ref, k_ref, v_ref, z_ref, ab_ref, wq_ref, wk_ref, wv_ref, alog_ref, dtb_ref, gain_ref,
               o_ref, qs, ks, vs, gb, bb, u_scr, wq_scr, ka_scr, cd_scr, o_scr, *, ctx_len, total):
    h = pl.program_id(1)
    n_chunks = total // CHUNK
    ctx_chunks = ctx_len // CHUNK
    tr = ROW_TILE
    lane = lax.broadcasted_iota(jnp.int32, (1, LANES), 1)
    rowi = lax.broadcasted_iota(jnp.int32, (tr, 1), 0)

    def short_conv(ref, w_ref, r0):
        x = ref[0, r0:r0 + tr, :].astype(F32)
        zero = jnp.zeros((1, LANES), F32)
        prev_row = zero if r0 in (0, ctx_len) else ref[0, r0 - 1:r0, :].astype(F32)
        next_row = zero if r0 + tr in (ctx_len, total) else ref[0, r0 + tr:r0 + tr + 1, :].astype(F32)
        xp = jnp.where(rowi == 0, prev_row, pltpu.roll(x, 1, 0))
        xn = jnp.where(rowi == tr - 1, next_row, pltpu.roll(x, tr - 1, 0))
        y = w_ref[0:1, :] * xp + w_ref[1:2, :] * x + w_ref[2:3, :] * xn
        return _silu(y)

    def l2norm(t):
        return t * lax.rsqrt(jnp.sum(t * t, axis=1, keepdims=True) + RMS_EPS)

    def lane_pick(t, idx):
        return jnp.broadcast_to(jnp.sum(jnp.where(lane == idx, t, 0.0), axis=1, keepdims=True), t.shape)

    for r0 in range(0, total, tr):
        qs[r0:r0 + tr, :] = l2norm(short_conv(q_ref, wq_ref, r0)) * (DN_HEAD_DIM ** -0.5)
        ks[r0:r0 + tr, :] = l2norm(short_conv(k_ref, wk_ref, r0))
        vs[r0:r0 + tr, :] = short_conv(v_ref, wv_ref, r0)
        a = ab_ref[0, r0:r0 + tr, :]
        pre = a + dtb_ref[...]
        softplus = jnp.maximum(pre, 0.0) + jnp.log1p(jnp.exp(-jnp.abs(pre)))
        g_all = -jnp.exp(alog_ref[...]) * softplus
        b_all = jax.nn.sigmoid(a)
        for d in range(2):
            gb[d, r0:r0 + tr, :] = lane_pick(g_all, d * DN_HEADS + h)
            bb[d, r0:r0 + tr, :] = lane_pick(b_all, 2 * DN_HEADS + d * DN_HEADS + h)

    ii = lax.broadcasted_iota(jnp.int32, (CHUNK, CHUNK), 0)
    jj = lax.broadcasted_iota(jnp.int32, (CHUNK, CHUNK), 1)
    incl = (ii >= jj, ii <= jj)
    strict = (ii > jj, ii < jj)
    tri = tuple(jnp.where(m, 1.0, 0.0).astype(BF16) for m in incl)
    eye = jnp.where(ii == jj, 1.0, 0.0).astype(F32)
    pair_masks = tuple(jnp.logical_and(ii // (2 * s) == jj // (2 * s), ii // s != jj // s)
                       for s in (1, 2, 4, 8, 16, 32))

    def local(c, carry):
        rows = pl.ds(pl.multiple_of(c * CHUNK, CHUNK), CHUNK)
        k = ks[rows, :]
        q = qs[rows, :]
        v = vs[rows, :]
        kq = jnp.concatenate([k, q], axis=0).astype(BF16)
        kkqk = lax.dot_general(kq, k.astype(BF16), (((1,), (1,)), ((), ())), preferred_element_type=F32)
        kk = kkqk[:CHUNK]
        qk = kkqk[CHUNK:]
        for d in range(2):
            g = gb[d, rows, :]
            be = bb[d, rows, :]
            g_hi = g.astype(BF16)
            g_lo = (g - g_hi.astype(F32)).astype(BF16)
            gcol = (jnp.dot(tri[d], g_hi, preferred_element_type=F32)
                    + jnp.dot(tri[d], g_lo, preferred_element_type=F32))
            grow = gcol.T[:CHUNK, :]
            dm = jnp.exp(jnp.where(incl[d], gcol[:, :CHUNK] - grow, NEG_INF))
            a_mat = jnp.where(strict[d], kk * dm, 0.0) * be[:, :CHUNK]
            inv = _unit_triangular_inverse(a_mat, eye, pair_masks)
            ek = jnp.exp(gcol)
            rhs = jnp.concatenate([v * be, k * be * ek], axis=1).astype(BF16)
            uw = jnp.dot(inv.astype(BF16), rhs, preferred_element_type=F32)
            g_last = gcol[CHUNK - 1:CHUNK, :] if d == 0 else gcol[0:1, :]
            k_dec = k * jnp.exp(g_last - gcol)
            u_scr[d, rows, :] = uw[:, :LANES]
            wq_scr[d, c, 0:CHUNK, :] = uw[:, LANES:].astype(BF16)
            wq_scr[d, c, CHUNK:, :] = (q * ek).astype(BF16)
            ka_scr[d, c, 0:LANES, :] = k_dec.T.astype(BF16)
            ka_scr[d, c, LANES:, :] = jnp.where(incl[d], qk * dm, 0.0).astype(BF16)
            cd_scr[d, c] = jnp.broadcast_to(jnp.exp(g_last), (8, LANES))
        return carry

    lax.fori_loop(0, n_chunks, local, 0)

    def step(t, states):
        new_states = []
        for d in range(2):
            if d == 0:
                c = t
            else:
                c = jnp.where(t < ctx_chunks, ctx_chunks - 1 - t, n_chunks - 1 + ctx_chunks - t)
            rows = pl.ds(pl.multiple_of(c * CHUNK, CHUNK), CHUNK)
            s = states[d]
            ws = jnp.dot(wq_scr[d, c], s.astype(BF16), preferred_element_type=F32)
            v_new = u_scr[d, rows, :] - ws[:CHUNK]
            ka = jnp.dot(ka_scr[d, c], v_new.astype(BF16), preferred_element_type=F32)
            o_scr[d, rows, :] = ws[CHUNK:] + ka[LANES:]
            new_states.append(s * cd_scr[d, c][0:1, :] + ka[:LANES])
        return tuple(new_states)

    zero_state = jnp.zeros((DN_HEAD_DIM, DN_HEAD_DIM), F32)
    lax.fori_loop(0, n_chunks, step, (zero_state, zero_state))

    for r0 in range(0, total, tr):
        o = o_scr[0, r0:r0 + tr, :] + o_scr[1, r0:r0 + tr, :]
        o = o * lax.rsqrt(jnp.mean(o * o, axis=1, keepdims=True) + RMS_EPS) * gain_ref[...]
        o_ref[0, r0:r0 + tr, :] = o * _silu(z_ref[0, r0:r0 + tr, :].astype(F32))


def _dn_call(p, ab, conv_w, a_log, dt_bias, gain, ctx_len):
    B, T, _ = p.shape
    nc = T // CHUNK

    def pad_row(v):
        return jnp.zeros((1, LANES), F32).at[0, :v.size].set(v.reshape(-1))

    def col(off):
        return lambda b, h: (b, 0, off // LANES + h)

    seq = lambda off: pl.BlockSpec((1, T, LANES), col(off))
    tap = lambda base: pl.BlockSpec((3, LANES), lambda b, h: (0, base + h))
    vec = pl.BlockSpec((1, LANES), lambda b, h: (0, 0))
    return pl.pallas_call(
        functools.partial(_dn_kernel, ctx_len=ctx_len, total=T),
        grid=(B, DN_HEADS),
        in_specs=[
            seq(OFF_CQ), seq(OFF_CK), seq(OFF_CV), seq(OFF_CZ),
            pl.BlockSpec((1, T, LANES), lambda b, h: (b, 0, 0)),
            tap(0), tap(DN_HEADS), tap(2 * DN_HEADS),
            vec, vec, vec,
        ],
        out_specs=pl.BlockSpec((1, T, LANES), lambda b, h: (b, 0, h)),
        out_shape=jax.ShapeDtypeStruct((B, T, DN_WIDTH), F32),
        scratch_shapes=[
            pltpu.VMEM((T, LANES), F32), pltpu.VMEM((T, LANES), F32), pltpu.VMEM((T, LANES), F32),
            pltpu.VMEM((2, T, LANES), F32), pltpu.VMEM((2, T, LANES), F32),
            pltpu.VMEM((2, T, LANES), F32),
            pltpu.VMEM((2, nc, 2 * CHUNK, LANES), BF16),
            pltpu.VMEM((2, nc, LANES + CHUNK, CHUNK), BF16),
            pltpu.VMEM((2, nc, 8, LANES), F32),
            pltpu.VMEM((2, T, LANES), F32),
        ],
        compiler_params=_params("arbitrary", "arbitrary"),
        name="gated_deltanet",
    )(p, p, p, p, ab, conv_w, conv_w, conv_w, pad_row(a_log), pad_row(dt_bias), gain.reshape(1, LANES))


def _merge_kernel(ya_ref, yb_ref, yc_ref, g0_ref, g1_ref, g2_ref, x_ref, mb_ref, mc_ref, wb_ref, wo_ref,
                  lg_ref, lb_ref, o_ref, *, ctx_tiles):
    i = pl.program_id(1)

    def branch(y_ref, g_ref, r):
        proj = jnp.dot(y_ref[0].astype(BF16), wb_ref[r], preferred_element_type=F32)
        return jax.nn.sigmoid(g_ref[0].astype(F32)) * proj

    merged = branch(ya_ref, g0_ref, 0) + branch(yb_ref, g1_ref, 1) + branch(yc_ref, g2_ref, 2)
    out = jnp.dot(merged.astype(BF16), wo_ref[...], preferred_element_type=F32)
    gate = jnp.where(i < ctx_tiles, mc_ref[0, 2:3, :], mb_ref[0, 2:3, :])
    t = DEEPNORM_ALPHA * x_ref[0] + gate * out
    mu = jnp.mean(t, axis=1, keepdims=True)
    cen = t - mu
    var = jnp.mean(cen * cen, axis=1, keepdims=True)
    o_ref[0] = cen * lax.rsqrt(var + LN_EPS) * lg_ref[...] + lb_ref[...]


def _merge_call(ya, yb, yc, p, xc, mods, w_branch, w_out, ln_g, ln_b, ctx_len):
    B, T, _ = xc.shape
    ctx_row = B
    gcol = OFF_GATE // D_MODEL
    y_spec = pl.BlockSpec((1, ROW_TILE, ATT_WIDTH), lambda b, i: (b, i, 0))
    gate_spec = lambda r: pl.BlockSpec((1, ROW_TILE, D_MODEL), lambda b, i: (b, i, gcol + r))
    vec = pl.BlockSpec((1, D_MODEL), lambda b, i: (0, 0))
    return pl.pallas_call(
        functools.partial(_merge_kernel, ctx_tiles=ctx_len // ROW_TILE),
        grid=(B, T // ROW_TILE),
        in_specs=[
            y_spec, y_spec, y_spec,
            gate_spec(0), gate_spec(1), gate_spec(2),
            pl.BlockSpec((1, ROW_TILE, D_MODEL), lambda b, i: (b, i, 0)),
            pl.BlockSpec((1, 3, D_MODEL), lambda b, i: (b, 0, 0)),
            pl.BlockSpec((1, 3, D_MODEL), lambda b, i: (ctx_row, 0, 0)),
            pl.BlockSpec((N_BRANCH, ATT_WIDTH, D_MODEL), lambda b, i: (0, 0, 0)),
            pl.BlockSpec((D_MODEL, D_MODEL), lambda b, i: (0, 0)),
            vec, vec,
        ],
        out_specs=pl.BlockSpec((1, ROW_TILE, D_MODEL), lambda b, i: (b, i, 0)),
        out_shape=jax.ShapeDtypeStruct((B, T, D_MODEL), F32),
        compiler_params=_params("arbitrary", "arbitrary"),
        name="merge",
    )(ya, yb, yc, p, p, p, xc, mods, mods, w_branch, w_out, ln_g.reshape(1, D_MODEL), ln_b.reshape(1, D_MODEL))


def _permute_in_proj(w_in):
    pieces = [
        w_in[..., 0:512], w_in[..., 768:1280],
        w_in[..., 1280:2304], w_in[..., 2304:2816],
        w_in[..., 4352:4864],
        w_in[..., 4880:7952],
        w_in[..., 2816:4352],
        w_in[..., 512:640], w_in[..., 640:768],
        w_in[..., 4864:4880],
    ]
    pad = IN_WIDTH_PAD - sum(piece.shape[-1] for piece in pieces)
    pieces.append(jnp.zeros(w_in.shape[:-1] + (pad,), w_in.dtype))
    return jnp.concatenate(pieces, axis=-1).astype(BF16)


def kernel(x, c, ctx, c_ctx, w_ada, b_ada, w_in, a_sink, b_conv_w, b_conv_b, b_norm_g, b_norm_b,
           c_conv_w, c_a_log, c_dt_bias, c_norm_g, w_branch, w_out, ln_g, ln_b):
    B, n, _ = x.shape
    ctx_len = ctx.shape[1]
    assert ctx_len % ROW_TILE == 0 and n % ROW_TILE == 0 and n % GRID_W == 0

    xc = jnp.concatenate([ctx, x], axis=1)
    mod_rows = -(-(B + 1) // 8) * 8
    cc = jnp.concatenate([c, c_ctx[None, :], jnp.zeros((mod_rows - B - 1, D_MODEL), F32)], axis=0)
    mods_all = _ada_call(cc, w_ada, b_ada).reshape(DEPTH, mod_rows, 3, D_MODEL)
    w_perm = _permute_in_proj(w_in)
    wb16 = w_branch.astype(BF16)
    wo16 = w_out.astype(BF16)
    cos, sin = _rope_tables(n, ctx_len)

    for l in range(DEPTH):
        mods = mods_all[l]
        p, ab = _inproj_call(xc, mods, w_perm[l], ctx_len, F32)
        ya = _attn_call(p, a_sink[l], cos, sin, ctx_len)
        yb = _conv_call(p, b_conv_w[l], b_conv_b[l], b_norm_g[l], b_norm_b[l], ctx_len)
        yc = _dn_call(p, ab, c_conv_w[l], c_a_log[l], c_dt_bias[l], c_norm_g[l], ctx_len)
        xc = _merge_call(ya, yb, yc, p, xc, mods, wb16[l], wo16[l], ln_g[l], ln_b[l], ctx_len)
    return xc[:, ctx_len:]
```

```python
import functools

import jax
import jax.numpy as jnp
from jax import lax
from jax.experimental import pallas as pl
from jax.experimental.pallas import tpu as pltpu

F32 = jnp.float32
BF16 = jnp.bfloat16
HIGHEST = lax.Precision.HIGHEST

LANES = 128
VMEM_LIMIT = 56 * 1024 * 1024

D_MODEL = 1024
DEPTH = 4
GRID_W = 64
HEAD_DIM = 64
ATT_Q_HEADS = 8
ATT_KV_HEADS = 2
ATT_GROUP = ATT_Q_HEADS // ATT_KV_HEADS
ATT_WIDTH = 512
BLOCK = 128
ROPE_BASE = 10000.0
ROPE_PAIRS = HEAD_DIM // 4
CONV_WIDTH = 512
CONV_TAPS = 31
CONV_HALO = 16
DN_HEADS = 4
DN_HEAD_DIM = 128
DN_WIDTH = 512
CHUNK = 64
N_BRANCH = 3
DEEPNORM_ALPHA = (2 * DEPTH) ** 0.25
LN_EPS = 1e-5
RMS_EPS = 1e-6
NEG_INF = -1e30

OFF_AQ, OFF_AZ = 0, 512
OFF_BA, OFF_BB, OFF_BZ = 1024, 1536, 2048
OFF_CZ = 2560
OFF_GATE = 3072
OFF_CQ, OFF_CK, OFF_CV = 6144, 6656, 7168
OFF_AK, OFF_AV = 7680, 7808
OFF_AB = 7936
IN_WIDTH_PAD = 8064
IN_TILE_N = 1152
ROW_TILE = 256


def _silu(t):
    return t * jax.nn.sigmoid(t)


def _params(*sem):
    return pltpu.CompilerParams(dimension_semantics=sem, vmem_limit_bytes=VMEM_LIMIT)


def _ada_kernel(cc_ref, w_ref, b_ref, o_ref):
    s = _silu(cc_ref[...])
    o_ref[0] = jnp.dot(s, w_ref[0], precision=HIGHEST, preferred_element_type=F32) + b_ref[0]


def _ada_call(cc, w_ada, b_ada):
    rows = cc.shape[0]
    return pl.pallas_call(
        _ada_kernel,
        grid=(DEPTH, 3),
        in_specs=[
            pl.BlockSpec((rows, D_MODEL), lambda l, j: (0, 0)),
            pl.BlockSpec((1, D_MODEL, D_MODEL), lambda l, j: (l, 0, j)),
            pl.BlockSpec((1, 1, D_MODEL), lambda l, j: (l, 0, j)),
        ],
        out_specs=pl.BlockSpec((1, rows, D_MODEL), lambda l, j: (l, 0, j)),
        out_shape=jax.ShapeDtypeStruct((DEPTH, rows, 3 * D_MODEL), F32),
        compiler_params=_params("arbitrary", "arbitrary"),
        name="ada_mod",
    )(cc, w_ada, b_ada.reshape(DEPTH, 1, 3 * D_MODEL))


def _inproj_kernel(x_ref, mb_ref, mc_ref, w_ref, p_ref, ab_ref, u_scr, *, tm, ctx_len, nj):
    i = pl.program_id(1)
    j = pl.program_id(2)

    @pl.when(j == 0)
    def _():
        row = lax.broadcasted_iota(jnp.int32, (tm, 1), 0) + i * tm
        is_ctx = row < ctx_len
        shift = jnp.where(is_ctx, mc_ref[0, 0:1, :], mb_ref[0, 0:1, :])
        scale = jnp.where(is_ctx, mc_ref[0, 1:2, :], mb_ref[0, 1:2, :])
        u_scr[...] = (x_ref[0] * (1.0 + scale) + shift).astype(BF16)

    p = jnp.dot(u_scr[...], w_ref[...], preferred_element_type=F32)
    p_ref[0] = p.astype(p_ref.dtype)

    @pl.when(j == nj - 1)
    def _():
        ab_ref[0] = p[:, IN_TILE_N - LANES:]


def _inproj_call(xc, mods, w, ctx_len, p_dtype):
    B, T, _ = xc.shape
    tm = T // 2
    nj = IN_WIDTH_PAD // IN_TILE_N
    ctx_row = B
    return pl.pallas_call(
        functools.partial(_inproj_kernel, tm=tm, ctx_len=ctx_len, nj=nj),
        grid=(B, T // tm, nj),
        in_specs=[
            pl.BlockSpec((1, tm, D_MODEL), lambda b, i, j: (b, i, 0)),
            pl.BlockSpec((1, 3, D_MODEL), lambda b, i, j: (b, 0, 0)),
            pl.BlockSpec((1, 3, D_MODEL), lambda b, i, j: (ctx_row, 0, 0)),
            pl.BlockSpec((D_MODEL, IN_TILE_N), lambda b, i, j: (0, j)),
        ],
        out_specs=[
            pl.BlockSpec((1, tm, IN_TILE_N), lambda b, i, j: (b, i, j)),
            pl.BlockSpec((1, tm, LANES), lambda b, i, j: (b, i, 0)),
        ],
        out_shape=[
            jax.ShapeDtypeStruct((B, T, IN_WIDTH_PAD), p_dtype),
            jax.ShapeDtypeStruct((B, T, LANES), F32),
        ],
        scratch_shapes=[pltpu.VMEM((tm, D_MODEL), BF16)],
        compiler_params=_params("arbitrary", "arbitrary", "arbitrary"),
        name="in_proj",
    )(xc, mods, mods, w)


def _rope(t, cos, sin_signed):
    lane = lax.broadcasted_iota(jnp.int32, (1, LANES), 1)
    partner = jnp.where((lane % 32) < 16, pltpu.roll(t, LANES - 16, 1), pltpu.roll(t, 16, 1))
    return t * cos + partner * sin_signed


def _attn_kernel(sink_ref, q_ref, z_ref, k_ref, v_ref, cq_ref, sq_ref, ck_ref, sk_ref, o_ref, kr_scr,
                 *, ctx_len, n_blocks):
    i = pl.program_id(1)
    ctx_blocks = ctx_len // BLOCK

    @pl.when(i == 0)
    def _():
        kr_scr[...] = _rope(k_ref[0].astype(F32), ck_ref[...], sk_ref[...]).astype(BF16)

    q = q_ref[0].astype(F32)
    cq = cq_ref[...]
    sq = sq_ref[...]
    q_pairs = [_rope(q[:, LANES * t:LANES * (t + 1)], cq, sq) for t in range(ATT_WIDTH // LANES)]

    def q_head(hq):
        off = (hq % 2) * HEAD_DIM
        return q_pairs[hq // 2][:, off:off + HEAD_DIM]

    left = pl.multiple_of(jnp.maximum(i - 1, 0) * BLOCK, BLOCK)
    mid = pl.multiple_of(i * BLOCK, BLOCK)
    right = pl.multiple_of(jnp.minimum(i + 1, n_blocks - 1) * BLOCK, BLOCK)
    kcat = jnp.concatenate([kr_scr[0:ctx_len, :], kr_scr[pl.ds(left, BLOCK), :],
                            kr_scr[pl.ds(mid, BLOCK), :], kr_scr[pl.ds(right, BLOCK), :]], axis=0)
    vcat = jnp.concatenate([v_ref[0, 0:ctx_len, :], v_ref[0, pl.ds(left, BLOCK), :],
                            v_ref[0, pl.ds(mid, BLOCK), :], v_ref[0, pl.ds(right, BLOCK), :]],
                           axis=0).astype(BF16)

    nk = ctx_len + 3 * BLOCK
    rows = ATT_GROUP * BLOCK
    col = lax.broadcasted_iota(jnp.int32, (1, nk), 1)
    is_lat = i >= ctx_blocks
    has_left = i >= ctx_blocks + 1
    has_right = jnp.logical_and(is_lat, i <= n_blocks - 2)
    full = BLOCK - 1
    hi_left = jnp.where(has_left, col - ctx_len, -1)
    hi_mid = jnp.where(is_lat, full, -1)
    lo_right = jnp.where(has_right, col - ctx_len - 2 * BLOCK, BLOCK)
    hi = jnp.where(col < ctx_len, full,
                   jnp.where(col < ctx_len + BLOCK, hi_left,
                             jnp.where(col < ctx_len + 2 * BLOCK, hi_mid, full)))
    lo = jnp.where(col < ctx_len + 2 * BLOCK, 0, lo_right)
    rq = lax.broadcasted_iota(jnp.int32, (rows, 1), 0) % BLOCK
    grp = lax.broadcasted_iota(jnp.int32, (rows, 1), 0) // BLOCK
    valid = jnp.logical_and(rq >= lo, rq <= hi)

    scale = HEAD_DIM ** -0.5
    outs = []
    for h in range(ATT_KV_HEADS):
        qh = jnp.concatenate([q_head(h * ATT_GROUP + g) for g in range(ATT_GROUP)], axis=0).astype(BF16)
        kh = kcat[:, h * HEAD_DIM:(h + 1) * HEAD_DIM]
        vh = vcat[:, h * HEAD_DIM:(h + 1) * HEAD_DIM]
        s = lax.dot_general(qh, kh, (((1,), (1,)), ((), ())), preferred_element_type=F32) * scale
        s = jnp.where(valid, s, NEG_INF)
        sk = jnp.zeros((rows, 1), F32)
        for g in range(ATT_GROUP):
            sk = jnp.where(grp == g, sink_ref[h * ATT_GROUP + g], sk)
        m = jnp.maximum(jnp.max(s, axis=1, keepdims=True), sk)
        p = jnp.exp(s - m)
        den = jnp.sum(p, axis=1, keepdims=True) + jnp.exp(sk - m)
        o = jnp.dot(p.astype(BF16), vh, preferred_element_type=F32) / den
        outs.extend(o[g * BLOCK:(g + 1) * BLOCK] for g in range(ATT_GROUP))
    att = jnp.concatenate(outs, axis=1)
    o_ref[0] = att * _silu(z_ref[0].astype(F32))


def _attn_call(p, sink, cos, sin, ctx_len):
    B, T, _ = p.shape
    nb = T // BLOCK
    return pl.pallas_call(
        functools.partial(_attn_kernel, ctx_len=ctx_len, n_blocks=nb),
        grid=(B, nb),
        in_specs=[
            pl.BlockSpec(memory_space=pltpu.SMEM),
            pl.BlockSpec((1, BLOCK, ATT_WIDTH), lambda b, i: (b, i, OFF_AQ // ATT_WIDTH)),
            pl.BlockSpec((1, BLOCK, ATT_WIDTH), lambda b, i: (b, i, OFF_AZ // ATT_WIDTH)),
            pl.BlockSpec((1, T, LANES), lambda b, i: (b, 0, OFF_AK // LANES)),
            pl.BlockSpec((1, T, LANES), lambda b, i: (b, 0, OFF_AV // LANES)),
            pl.BlockSpec((BLOCK, LANES), lambda b, i: (i, 0)),
            pl.BlockSpec((BLOCK, LANES), lambda b, i: (i, 0)),
            pl.BlockSpec((T, LANES), lambda b, i: (0, 0)),
            pl.BlockSpec((T, LANES), lambda b, i: (0, 0)),
        ],
        out_specs=pl.BlockSpec((1, BLOCK, ATT_WIDTH), lambda b, i: (b, i, 0)),
        out_shape=jax.ShapeDtypeStruct((B, T, ATT_WIDTH), F32),
        scratch_shapes=[pltpu.VMEM((T, LANES), BF16)],
        compiler_params=_params("arbitrary", "arbitrary"),
        name="attention",
    )(sink, p, p, p, p, cos, sin, cos, sin)


def _rope_tables(n, ctx_len):
    pos = jnp.arange(n)
    row = (pos // GRID_W).astype(F32)
    colp = (pos % GRID_W).astype(F32)
    inv = ROPE_BASE ** (-jnp.arange(ROPE_PAIRS, dtype=F32) / ROPE_PAIRS)
    lane = jnp.arange(LANES)
    axis = (lane % HEAD_DIM) // (2 * ROPE_PAIRS)
    pair = lane % ROPE_PAIRS
    ang = jnp.where(axis[None, :] == 0, row[:, None], colp[:, None]) * inv[pair][None, :]
    sign = jnp.where((lane % (2 * ROPE_PAIRS)) < ROPE_PAIRS, -1.0, 1.0).astype(F32)
    cos = jnp.concatenate([jnp.ones((ctx_len, LANES), F32), jnp.cos(ang)], axis=0)
    sin = jnp.concatenate([jnp.zeros((ctx_len, LANES), F32), jnp.sin(ang) * sign[None, :]], axis=0)
    return cos, sin


def _conv_kernel(a_ref, b_ref, z_ref, ap_ref, bp_ref, an_ref, bn_ref, w_ref, cb_ref, g_ref, nb_ref,
                 o_ref, h_scr, *, ctx_tiles, n_tiles):
    i = pl.program_id(1)

    def glu(a, b):
        return a.astype(F32) * jax.nn.sigmoid(b.astype(F32))

    has_prev = jnp.logical_and(i != 0, i != ctx_tiles)
    has_next = jnp.logical_and(i != ctx_tiles - 1, i != n_tiles - 1)
    h_scr[0:CONV_HALO, :] = jnp.where(has_prev, glu(ap_ref[0], bp_ref[0]), 0.0)
    h_scr[CONV_HALO:CONV_HALO + ROW_TILE, :] = glu(a_ref[0], b_ref[0])
    h_scr[CONV_HALO + ROW_TILE:, :] = jnp.where(has_next, glu(an_ref[0], bn_ref[0]), 0.0)

    base = CONV_HALO - CONV_TAPS // 2
    acc = jnp.zeros((ROW_TILE, CONV_WIDTH), F32)
    for k in range(CONV_TAPS):
        acc = acc + w_ref[k:k + 1, :] * h_scr[base + k:base + k + ROW_TILE, :]
    acc = acc + cb_ref[...]
    mu = jnp.mean(acc, axis=1, keepdims=True)
    cen = acc - mu
    var = jnp.mean(cen * cen, axis=1, keepdims=True)
    hn = cen * lax.rsqrt(var + LN_EPS) * g_ref[...] + nb_ref[...]
    o_ref[0] = _silu(hn) * _silu(z_ref[0].astype(F32))


def _conv_call(p, conv_w, conv_b, norm_g, norm_b, ctx_len):
    B, T, _ = p.shape
    nt = T // ROW_TILE
    hpt = ROW_TILE // CONV_HALO
    last_halo = T // CONV_HALO - 1
    ca, cb_, cz = OFF_BA // CONV_WIDTH, OFF_BB // CONV_WIDTH, OFF_BZ // CONV_WIDTH

    def prev_map(col):
        return lambda b, i: (b, jnp.maximum(i * hpt - 1, 0), col)

    def next_map(col):
        return lambda b, i: (b, jnp.minimum((i + 1) * hpt, last_halo), col)

    row = lambda v: v.reshape(1, CONV_WIDTH)
    vec_spec = pl.BlockSpec((1, CONV_WIDTH), lambda b, i: (0, 0))
    return pl.pallas_call(
        functools.partial(_conv_kernel, ctx_tiles=ctx_len // ROW_TILE, n_tiles=nt),
        grid=(B, nt),
        in_specs=[
            pl.BlockSpec((1, ROW_TILE, CONV_WIDTH), lambda b, i: (b, i, ca)),
            pl.BlockSpec((1, ROW_TILE, CONV_WIDTH), lambda b, i: (b, i, cb_)),
            pl.BlockSpec((1, ROW_TILE, CONV_WIDTH), lambda b, i: (b, i, cz)),
            pl.BlockSpec((1, CONV_HALO, CONV_WIDTH), prev_map(ca)),
            pl.BlockSpec((1, CONV_HALO, CONV_WIDTH), prev_map(cb_)),
            pl.BlockSpec((1, CONV_HALO, CONV_WIDTH), next_map(ca)),
            pl.BlockSpec((1, CONV_HALO, CONV_WIDTH), next_map(cb_)),
            pl.BlockSpec((CONV_TAPS, CONV_WIDTH), lambda b, i: (0, 0)),
            vec_spec, vec_spec, vec_spec,
        ],
        out_specs=pl.BlockSpec((1, ROW_TILE, CONV_WIDTH), lambda b, i: (b, i, 0)),
        out_shape=jax.ShapeDtypeStruct((B, T, CONV_WIDTH), F32),
        scratch_shapes=[pltpu.VMEM((ROW_TILE + 2 * CONV_HALO, CONV_WIDTH), F32)],
        compiler_params=_params("arbitrary", "arbitrary"),
        name="conformer_conv",
    )(p, p, p, p, p, p, p, conv_w, row(conv_b), row(norm_g), row(norm_b))


DN_PACK = 4
DN_GROUP = 3


def _split_bf16(t):
    hi = t.astype(BF16)
    return hi, (t - hi.astype(F32)).astype(BF16)


def _dot(a, b):
    return jnp.dot(a, b, preferred_element_type=F32)


def _dn_kernel(q_ref, k_ref, v_ref, z_ref, ab_ref, wq_ref, wk_ref, wv_ref, alog_ref, dtb_ref, gain_ref,
               o_ref, qs, ks, vs, kts, gb, bb, grow, u_scr, wq_scr, ka_scr, cd_scr, o_scr, *, ctx_len, total):
    h = pl.program_id(1)
    n_chunks = total // CHUNK
    n_pairs = n_chunks // 2
    ctx_chunks = ctx_len // CHUNK
    tr = ROW_TILE
    wide = DN_PACK * CHUNK
    lane = lax.broadcasted_iota(jnp.int32, (1, LANES), 1)
    rowi = lax.broadcasted_iota(jnp.int32, (tr, 1), 0)

    def short_conv(ref, w_ref, r0):
        x = ref[0, r0:r0 + tr, :].astype(F32)
        zero = jnp.zeros((1, LANES), F32)
        prev_row = zero if r0 in (0, ctx_len) else ref[0, r0 - 1:r0, :].astype(F32)
        next_row = zero if r0 + tr in (ctx_len, total) else ref[0, r0 + tr:r0 + tr + 1, :].astype(F32)
        xp = jnp.where(rowi == 0, prev_row, pltpu.roll(x, 1, 0))
        xn = jnp.where(rowi == tr - 1, next_row, pltpu.roll(x, tr - 1, 0))
        y = w_ref[0:1, :] * xp + w_ref[1:2, :] * x + w_ref[2:3, :] * xn
        return _silu(y)

    def l2norm(t):
        return t * lax.rsqrt(jnp.sum(t * t, axis=1, keepdims=True) + RMS_EPS)

    def lane_pick(t, idx):
        return jnp.broadcast_to(jnp.sum(jnp.where(lane == idx, t, 0.0), axis=1, keepdims=True), t.shape)

    for r0 in range(0, total, tr):
        qs[r0:r0 + tr, :] = l2norm(short_conv(q_ref, wq_ref, r0)) * (DN_HEAD_DIM ** -0.5)
        kn = l2norm(short_conv(k_ref, wk_ref, r0))
        ks[r0:r0 + tr, :] = kn
        kts[:, r0:r0 + tr] = kn.T
        vs[r0:r0 + tr, :] = short_conv(v_ref, wv_ref, r0)
        a = ab_ref[0, r0:r0 + tr, :]
        pre = a + dtb_ref[...]
        softplus = jnp.maximum(pre, 0.0) + jnp.log1p(jnp.exp(-jnp.abs(pre)))
        g_all = -jnp.exp(alog_ref[...]) * softplus
        b_all = jax.nn.sigmoid(a)
        for d in range(2):
            g_sel = lane_pick(g_all, d * DN_HEADS + h)
            gb[d, r0:r0 + tr, :] = g_sel
            grow[d, :, r0:r0 + tr] = g_sel.T[0:8, :]
            bb[d, r0:r0 + tr, :] = lane_pick(b_all, 2 * DN_HEADS + d * DN_HEADS + h)

    prow = lax.broadcasted_iota(jnp.int32, (CHUNK, wide), 0)
    plane = lax.broadcasted_iota(jnp.int32, (CHUNK, wide), 1)
    pcol = plane % CHUNK
    ahead = jnp.where(plane < 2 * CHUNK, prow - pcol, pcol - prow)
    incl = ahead >= 0
    strict = ahead > 0
    eye = jnp.where(prow == pcol, 1.0, 0.0).astype(F32)
    level_masks = tuple(jnp.logical_and(prow // (2 * s) == pcol // (2 * s), prow // s != pcol // s)
                        for s in (1, 2, 4, 8, 16, 32))
    slot_masks = tuple(jnp.where(plane // CHUNK == i, 1.0, 0.0).astype(BF16) for i in range(DN_PACK))
    ci = lax.broadcasted_iota(jnp.int32, (CHUNK, CHUNK), 0)
    cj = lax.broadcasted_iota(jnp.int32, (CHUNK, CHUNK), 1)
    lower_ones = jnp.where(cj <= ci, 1.0, 0.0).astype(BF16)
    wr = lax.broadcasted_iota(jnp.int32, (wide, wide), 0)
    wc = lax.broadcasted_iota(jnp.int32, (wide, wide), 1)
    same_slot = wr // CHUNK == wc // CHUNK
    along = jnp.where(wr < 2 * CHUNK, wc - wr, wr - wc) >= 0
    row_cumsum = jnp.where(jnp.logical_and(same_slot, along), 1.0, 0.0).astype(BF16)
    left_half = lax.broadcasted_iota(jnp.int32, (CHUNK, LANES), 1) < CHUNK
    left_half_row = lane < CHUNK

    def block_diag(y):
        return jnp.concatenate([y * m for m in slot_masks], axis=0)

    def pack(t0, t1, t2, t3, mask=left_half):
        return jnp.concatenate([jnp.where(mask, t0, t1), jnp.where(mask, t2, t3)], axis=1)

    def local(gi, carry):
        pairs = [gi * DN_GROUP + j for j in range(DN_GROUP)]
        stage = []
        for p in pairs:
            rows = [pl.ds(pl.multiple_of(p * 2 * CHUNK + i * CHUNK, CHUNK), CHUNK) for i in range(2)]
            cols = pl.ds(pl.multiple_of(p * LANES, LANES), LANES)
            k = [ks[r, :] for r in rows]
            q = [qs[r, :] for r in rows]
            v = [vs[r, :] for r in rows]
            g = [gb[d, r, :] for d in range(2) for r in rows]
            be = [bb[d, r, :] for d in range(2) for r in rows]
            g_hi, g_lo = _split_bf16(jnp.concatenate(g, axis=1))
            prefix = _dot(lower_ones, g_hi) + _dot(lower_ones, g_lo)
            gcol, g_last = [], []
            for i in range(DN_PACK):
                pf = prefix[:, i * LANES:(i + 1) * LANES]
                tot = pf[CHUNK - 1:CHUNK, :]
                gcol.append(pf if i < 2 else tot - pf + g[i])
                g_last.append(tot)
            r_hi, r_lo = _split_bf16(jnp.concatenate(
                [jnp.broadcast_to(grow[d, 0:1, cols], (8, LANES)) for d in range(2)], axis=1))
            grow_cum = (_dot(r_hi, row_cumsum) + _dot(r_lo, row_cumsum))[0:1, :]
            dm = jnp.exp(jnp.where(incl, pack(*gcol) - grow_cum, NEG_INF))
            lhs = jnp.concatenate([jnp.concatenate([k[i], q[i]], axis=0) for i in range(2)], axis=1).astype(BF16)
            zeros = jnp.zeros((CHUNK, LANES), F32)
            rhs = jnp.concatenate([jnp.concatenate([k[0], zeros], axis=1),
                                   jnp.concatenate([zeros, k[1]], axis=1)], axis=0).astype(BF16)
            kkqk = lax.dot_general(lhs, rhs, (((1,), (1,)), ((), ())), preferred_element_type=F32)
            kk = jnp.concatenate([kkqk[:CHUNK], kkqk[:CHUNK]], axis=1)
            qk = jnp.concatenate([kkqk[CHUNK:], kkqk[CHUNK:]], axis=1)
            a_mat = jnp.where(strict, kk * dm, 0.0) * pack(*be)
            aqk = jnp.where(incl, qk * dm, 0.0)
            a_levels = [block_diag(jnp.where(m, a_mat, 0.0).astype(BF16)) for m in level_masks[1:]]
            inv = eye - jnp.where(level_masks[0], a_mat, 0.0)
            stage.append(dict(rows=rows, cols=cols, k=k, q=q, v=v, be=be, gcol=gcol, g_last=g_last,
                              grow_cum=grow_cum, aqk=aqk, a_levels=a_levels, inv=inv))

        for lvl in range(len(level_masks) - 1):
            inv_b = [st["inv"].astype(BF16) for st in stage]
            t1 = [_dot(ib, st["a_levels"][lvl]) for ib, st in zip(inv_b, stage)]
            t2 = [_dot(t.astype(BF16), block_diag(ib)) for t, ib in zip(t1, inv_b)]
            for st, t in zip(stage, t2):
                st["inv"] = st["inv"] - t

        outs = []
        for st in stage:
            k, q, v, be, gcol, g_last = st["k"], st["q"], st["v"], st["be"], st["gcol"], st["g_last"]
            ek = [jnp.exp(t) for t in gcol]
            rstack = jnp.concatenate(
                [jnp.concatenate([v[i % 2] * be[i], k[i % 2] * be[i] * ek[i]], axis=1) for i in range(DN_PACK)],
                axis=0).astype(BF16)
            uw = _dot(block_diag(st["inv"].astype(BF16)), rstack)
            e_row = jnp.exp(pack(*g_last, mask=left_half_row) - st["grow_cum"])
            kt = kts[:, st["cols"]]
            outs.append(dict(
                u=[uw[i * CHUNK:(i + 1) * CHUNK, :LANES] for i in range(DN_PACK)],
                wq=[jnp.concatenate([uw[i * CHUNK:(i + 1) * CHUNK, LANES:], q[i % 2] * ek[i]], axis=0).astype(BF16)
                    for i in range(DN_PACK)],
                ka=[jnp.concatenate([kt * e_row[:, d * LANES:(d + 1) * LANES],
                                     st["aqk"][:, d * LANES:(d + 1) * LANES]], axis=0).astype(BF16)
                    for d in range(2)],
                cd=[jnp.broadcast_to(jnp.exp(t), (8, LANES)) for t in g_last]))

        for p, st, out in zip(pairs, stage, outs):
            for i in range(DN_PACK):
                d, j = i // 2, i % 2
                u_scr[d, st["rows"][j], :] = out["u"][i]
                wq_scr[d, 2 * p + j] = out["wq"][i]
                cd_scr[d, 2 * p + j] = out["cd"][i]
            for d in range(2):
                ka_scr[d, p] = out["ka"][d]
        return carry

    lax.fori_loop(0, n_pairs // DN_GROUP, local, 0)

    def step(t, states):
        new_states = []
        for d in range(2):
            if d == 0:
                c = t
            else:
                c = jnp.where(t < ctx_chunks, ctx_chunks - 1 - t, n_chunks - 1 + ctx_chunks - t)
            rows = pl.ds(pl.multiple_of(c * CHUNK, CHUNK), CHUNK)
            s = states[d]
            ws = _dot(wq_scr[d, c], s.astype(BF16))
            v_new = u_scr[d, rows, :] - ws[:CHUNK]
            odd = (c % 2).astype(F32)
            v_rows = jnp.concatenate([v_new * (1.0 - odd), v_new * odd], axis=0).astype(BF16)
            ka = _dot(ka_scr[d, c // 2], v_rows)
            o_scr[d, rows, :] = ws[CHUNK:] + ka[LANES:]
            new_states.append(s * cd_scr[d, c][0:1, :] + ka[:LANES])
        return tuple(new_states)

    zero_state = jnp.zeros((DN_HEAD_DIM, DN_HEAD_DIM), F32)
    lax.fori_loop(0, n_chunks, step, (zero_state, zero_state))

    for r0 in range(0, total, tr):
        o = o_scr[0, r0:r0 + tr, :] + o_scr[1, r0:r0 + tr, :]
        o = o * lax.rsqrt(jnp.mean(o * o, axis=1, keepdims=True) + RMS_EPS) * gain_ref[...]
        o_ref[0, r0:r0 + tr, :] = o * _silu(z_ref[0, r0:r0 + tr, :].astype(F32))


def _dn_call(p, ab, conv_w, a_log, dt_bias, gain, ctx_len):
    B, T, _ = p.shape
    nc = T // CHUNK
    assert nc % (2 * DN_GROUP) == 0

    def pad_row(v):
        return jnp.zeros((1, LANES), F32).at[0, :v.size].set(v.reshape(-1))

    def col(off):
        return lambda b, h: (b, 0, off // LANES + h)

    seq = lambda off: pl.BlockSpec((1, T, LANES), col(off))
    tap = lambda base: pl.BlockSpec((3, LANES), lambda b, h: (0, base + h))
    vec = pl.BlockSpec((1, LANES), lambda b, h: (0, 0))
    return pl.pallas_call(
        functools.partial(_dn_kernel, ctx_len=ctx_len, total=T),
        grid=(B, DN_HEADS),
        in_specs=[
            seq(OFF_CQ), seq(OFF_CK), seq(OFF_CV), seq(OFF_CZ),
            pl.BlockSpec((1, T, LANES), lambda b, h: (b, 0, 0)),
            tap(0), tap(DN_HEADS), tap(2 * DN_HEADS),
            vec, vec, vec,
        ],
        out_specs=pl.BlockSpec((1, T, LANES), lambda b, h: (b, 0, h)),
        out_shape=jax.ShapeDtypeStruct((B, T, DN_WIDTH), F32),
        scratch_shapes=[
            pltpu.VMEM((T, LANES), F32), pltpu.VMEM((T, LANES), F32), pltpu.VMEM((T, LANES), F32),
            pltpu.VMEM((LANES, T), F32),
            pltpu.VMEM((2, T, LANES), F32), pltpu.VMEM((2, T, LANES), F32),
            pltpu.VMEM((2, 8, T), F32),
            pltpu.VMEM((2, T, LANES), F32),
            pltpu.VMEM((2, nc, 2 * CHUNK, LANES), BF16),
            pltpu.VMEM((2, nc // 2, LANES + CHUNK, LANES), BF16),
            pltpu.VMEM((2, nc, 8, LANES), F32),
            pltpu.VMEM((2, T, LANES), F32),
        ],
        compiler_params=_params("arbitrary", "arbitrary"),
        name="gated_deltanet",
    )(p, p, p, p, ab, conv_w, conv_w, conv_w, pad_row(a_log), pad_row(dt_bias), gain.reshape(1, LANES))


def _merge_kernel(ya_ref, yb_ref, yc_ref, g0_ref, g1_ref, g2_ref, x_ref, mb_ref, mc_ref, wb_ref, wo_ref,
                  lg_ref, lb_ref, o_ref, *, ctx_tiles):
    i = pl.program_id(1)

    def branch(y_ref, g_ref, r):
        proj = jnp.dot(y_ref[0].astype(BF16), wb_ref[r], preferred_element_type=F32)
        return jax.nn.sigmoid(g_ref[0].astype(F32)) * proj

    merged = branch(ya_ref, g0_ref, 0) + branch(yb_ref, g1_ref, 1) + branch(yc_ref, g2_ref, 2)
    out = jnp.dot(merged.astype(BF16), wo_ref[...], preferred_element_type=F32)
    gate = jnp.where(i < ctx_tiles, mc_ref[0, 2:3, :], mb_ref[0, 2:3, :])
    t = DEEPNORM_ALPHA * x_ref[0] + gate * out
    mu = jnp.mean(t, axis=1, keepdims=True)
    cen = t - mu
    var = jnp.mean(cen * cen, axis=1, keepdims=True)
    o_ref[0] = cen * lax.rsqrt(var + LN_EPS) * lg_ref[...] + lb_ref[...]


def _merge_call(ya, yb, yc, p, xc, mods, w_branch, w_out, ln_g, ln_b, ctx_len):
    B, T, _ = xc.shape
    ctx_row = B
    gcol = OFF_GATE // D_MODEL
    y_spec = pl.BlockSpec((1, ROW_TILE, ATT_WIDTH), lambda b, i: (b, i, 0))
    gate_spec = lambda r: pl.BlockSpec((1, ROW_TILE, D_MODEL), lambda b, i: (b, i, gcol + r))
    vec = pl.BlockSpec((1, D_MODEL), lambda b, i: (0, 0))
    return pl.pallas_call(
        functools.partial(_merge_kernel, ctx_tiles=ctx_len // ROW_TILE),
        grid=(B, T // ROW_TILE),
        in_specs=[
            y_spec, y_spec, y_spec,
            gate_spec(0), gate_spec(1), gate_spec(2),
            pl.BlockSpec((1, ROW_TILE, D_MODEL), lambda b, i: (b, i, 0)),
            pl.BlockSpec((1, 3, D_MODEL), lambda b, i: (b, 0, 0)),
            pl.BlockSpec((1, 3, D_MODEL), lambda b, i: (ctx_row, 0, 0)),
            pl.BlockSpec((N_BRANCH, ATT_WIDTH, D_MODEL), lambda b, i: (0, 0, 0)),
            pl.BlockSpec((D_MODEL, D_MODEL), lambda b, i: (0, 0)),
            vec, vec,
        ],
        out_specs=pl.BlockSpec((1, ROW_TILE, D_MODEL), lambda b, i: (b, i, 0)),
        out_shape=jax.ShapeDtypeStruct((B, T, D_MODEL), F32),
        compiler_params=_params("arbitrary", "arbitrary"),
        name="merge",
    )(ya, yb, yc, p, p, p, xc, mods, mods, w_branch, w_out, ln_g.reshape(1, D_MODEL), ln_b.reshape(1, D_MODEL))


def _permute_in_proj(w_in):
    pieces = [
        w_in[..., 0:512], w_in[..., 768:1280],
        w_in[..., 1280:2304], w_in[..., 2304:2816],
        w_in[..., 4352:4864],
        w_in[..., 4880:7952],
        w_in[..., 2816:4352],
        w_in[..., 512:640], w_in[..., 640:768],
        w_in[..., 4864:4880],
    ]
    pad = IN_WIDTH_PAD - sum(piece.shape[-1] for piece in pieces)
    pieces.append(jnp.zeros(w_in.shape[:-1] + (pad,), w_in.dtype))
    return jnp.concatenate(pieces, axis=-1).astype(BF16)


def kernel(x, c, ctx, c_ctx, w_ada, b_ada, w_in, a_sink, b_conv_w, b_conv_b, b_norm_g, b_norm_b,
           c_conv_w, c_a_log, c_dt_bias, c_norm_g, w_branch, w_out, ln_g, ln_b):
    B, n, _ = x.shape
    ctx_len = ctx.shape[1]
    assert ctx_len % ROW_TILE == 0 and n % ROW_TILE == 0 and n % GRID_W == 0

    xc = jnp.concatenate([ctx, x], axis=1)
    mod_rows = -(-(B + 1) // 8) * 8
    cc = jnp.concatenate([c, c_ctx[None, :], jnp.zeros((mod_rows - B - 1, D_MODEL), F32)], axis=0)
    mods_all = _ada_call(cc, w_ada, b_ada).reshape(DEPTH, mod_rows, 3, D_MODEL)
    w_perm = _permute_in_proj(w_in)
    wb16 = w_branch.astype(BF16)
    wo16 = w_out.astype(BF16)
    cos, sin = _rope_tables(n, ctx_len)

    for l in range(DEPTH):
        mods = mods_all[l]
        p, ab = _inproj_call(xc, mods, w_perm[l], ctx_len, F32)
        ya = _attn_call(p, a_sink[l], cos, sin, ctx_len)
        yb = _conv_call(p, b_conv_w[l], b_conv_b[l], b_norm_g[l], b_norm_b[l], ctx_len)
        yc = _dn_call(p, ab, c_conv_w[l], c_a_log[l], c_dt_bias[l], c_norm_g[l], ctx_len)
        xc = _merge_call(ya, yb, yc, p, xc, mods, wb16[l], wo16[l], ln_g[l], ln_b[l], ctx_len)
    return xc[:, ctx_len:]
```

```python
import functools

import jax
import jax.numpy as jnp
from jax import lax
from jax.experimental import pallas as pl
from jax.experimental.pallas import tpu as pltpu

F32 = jnp.float32
BF16 = jnp.bfloat16
HIGHEST = lax.Precision.HIGHEST

LANES = 128
VMEM_LIMIT = 56 * 1024 * 1024

D_MODEL = 1024
DEPTH = 4
GRID_W = 64
HEAD_DIM = 64
ATT_Q_HEADS = 8
ATT_KV_HEADS = 2
ATT_GROUP = ATT_Q_HEADS // ATT_KV_HEADS
ATT_WIDTH = 512
BLOCK = 128
ROPE_BASE = 10000.0
ROPE_PAIRS = HEAD_DIM // 4
CONV_WIDTH = 512
CONV_TAPS = 31
CONV_HALO = 16
DN_HEADS = 4
DN_HEAD_DIM = 128
DN_WIDTH = 512
CHUNK = 64
N_BRANCH = 3
DEEPNORM_ALPHA = (2 * DEPTH) ** 0.25
LN_EPS = 1e-5
RMS_EPS = 1e-6
NEG_INF = -1e30

OFF_AQ, OFF_AZ = 0, 512
OFF_BA, OFF_BB, OFF_BZ = 1024, 1536, 2048
OFF_CZ = 2560
OFF_GATE = 3072
OFF_CQ, OFF_CK, OFF_CV = 6144, 6656, 7168
OFF_AK, OFF_AV = 7680, 7808
OFF_AB = 7936
IN_WIDTH_PAD = 8192
IN_TILE_N = 1024
ROW_TILE = 256


def _silu(t):
    return t * jax.nn.sigmoid(t)


def _params(*sem):
    return pltpu.CompilerParams(dimension_semantics=sem, vmem_limit_bytes=VMEM_LIMIT)


def _ada_kernel(cc_ref, w_ref, b_ref, o_ref):
    s = _silu(cc_ref[...])
    o_ref[0] = jnp.dot(s, w_ref[0], precision=HIGHEST, preferred_element_type=F32) + b_ref[0]


def _ada_call(cc, w_ada, b_ada):
    rows = cc.shape[0]
    return pl.pallas_call(
        _ada_kernel,
        grid=(DEPTH, 3),
        in_specs=[
            pl.BlockSpec((rows, D_MODEL), lambda l, j: (0, 0)),
            pl.BlockSpec((1, D_MODEL, D_MODEL), lambda l, j: (l, 0, j)),
            pl.BlockSpec((1, 1, D_MODEL), lambda l, j: (l, 0, j)),
        ],
        out_specs=pl.BlockSpec((1, rows, D_MODEL), lambda l, j: (l, 0, j)),
        out_shape=jax.ShapeDtypeStruct((DEPTH, rows, 3 * D_MODEL), F32),
        compiler_params=_params("arbitrary", "arbitrary"),
        name="ada_mod",
    )(cc, w_ada, b_ada.reshape(DEPTH, 1, 3 * D_MODEL))


def _inproj_kernel(x_ref, mb_ref, mc_ref, w_ref, p_ref, ab_ref, u_scr, *, tm, ctx_len):
    i = pl.program_id(1)
    j = pl.program_id(2)

    @pl.when(j == 0)
    def _():
        row = lax.broadcasted_iota(jnp.int32, (tm, 1), 0) + i * tm
        is_ctx = row < ctx_len
        shift = jnp.where(is_ctx, mc_ref[0, 0:1, :], mb_ref[0, 0:1, :])
        scale = jnp.where(is_ctx, mc_ref[0, 1:2, :], mb_ref[0, 1:2, :])
        u_scr[...] = (x_ref[0] * (1.0 + scale) + shift).astype(BF16)

    p = jnp.dot(u_scr[...], w_ref[...], preferred_element_type=F32)
    p_ref[0] = p.astype(p_ref.dtype)

    @pl.when(j == OFF_AB // IN_TILE_N)
    def _():
        ab_ref[0] = p[:, OFF_AB % IN_TILE_N:OFF_AB % IN_TILE_N + LANES]


def _inproj_call(xc, mods, w, ctx_len, p_dtype):
    B, T, _ = xc.shape
    tm = T // 2
    nj = IN_WIDTH_PAD // IN_TILE_N
    ctx_row = B
    return pl.pallas_call(
        functools.partial(_inproj_kernel, tm=tm, ctx_len=ctx_len),
        grid=(B, T // tm, nj),
        in_specs=[
            pl.BlockSpec((1, tm, D_MODEL), lambda b, i, j: (b, i, 0)),
            pl.BlockSpec((1, 3, D_MODEL), lambda b, i, j: (b, 0, 0)),
            pl.BlockSpec((1, 3, D_MODEL), lambda b, i, j: (ctx_row, 0, 0)),
            pl.BlockSpec((D_MODEL, IN_TILE_N), lambda b, i, j: (0, j)),
        ],
        out_specs=[
            pl.BlockSpec((1, tm, IN_TILE_N), lambda b, i, j: (b, i, j)),
            pl.BlockSpec((1, tm, LANES), lambda b, i, j: (b, i, 0)),
        ],
        out_shape=[
            jax.ShapeDtypeStruct((B, T, IN_WIDTH_PAD), p_dtype),
            jax.ShapeDtypeStruct((B, T, LANES), F32),
        ],
        scratch_shapes=[pltpu.VMEM((tm, D_MODEL), BF16)],
        compiler_params=_params("arbitrary", "arbitrary", "arbitrary"),
        name="in_proj",
    )(xc, mods, mods, w)


def _rope(t, cos, sin_signed):
    lane = lax.broadcasted_iota(jnp.int32, (1, LANES), 1)
    partner = jnp.where((lane % 32) < 16, pltpu.roll(t, LANES - 16, 1), pltpu.roll(t, 16, 1))
    return t * cos + partner * sin_signed


def _attn_kernel(sink_ref, q_ref, z_ref, k_ref, v_ref, cq_ref, sq_ref, ck_ref, sk_ref, o_ref, kr_scr,
                 *, ctx_len, n_blocks):
    i = pl.program_id(1)
    ctx_blocks = ctx_len // BLOCK

    @pl.when(i == 0)
    def _():
        kr_scr[...] = _rope(k_ref[0].astype(F32), ck_ref[...], sk_ref[...]).astype(BF16)

    q = q_ref[0].astype(F32) * (HEAD_DIM ** -0.5)
    cq = cq_ref[...]
    sq = sq_ref[...]
    q_pairs = [_rope(q[:, LANES * t:LANES * (t + 1)], cq, sq) for t in range(ATT_WIDTH // LANES)]

    def q_head(hq):
        off = (hq % 2) * HEAD_DIM
        return q_pairs[hq // 2][:, off:off + HEAD_DIM]

    left = pl.multiple_of(jnp.maximum(i - 1, 0) * BLOCK, BLOCK)
    mid = pl.multiple_of(i * BLOCK, BLOCK)
    right = pl.multiple_of(jnp.minimum(i + 1, n_blocks - 1) * BLOCK, BLOCK)
    kcat = jnp.concatenate([kr_scr[0:ctx_len, :], kr_scr[pl.ds(left, BLOCK), :],
                            kr_scr[pl.ds(mid, BLOCK), :], kr_scr[pl.ds(right, BLOCK), :]], axis=0)
    vcat = jnp.concatenate([v_ref[0, 0:ctx_len, :], v_ref[0, pl.ds(left, BLOCK), :],
                            v_ref[0, pl.ds(mid, BLOCK), :], v_ref[0, pl.ds(right, BLOCK), :]],
                           axis=0).astype(BF16)

    rows = ATT_GROUP * BLOCK
    is_lat = i >= ctx_blocks
    has_left = i >= ctx_blocks + 1
    has_right = jnp.logical_and(is_lat, i <= n_blocks - 2)
    rq = lax.broadcasted_iota(jnp.int32, (rows, BLOCK), 0) % BLOCK
    kc = lax.broadcasted_iota(jnp.int32, (rows, BLOCK), 1)
    grp = lax.broadcasted_iota(jnp.int32, (rows, 1), 0) // BLOCK
    neg_left = jnp.where(has_left, 0.0, NEG_INF)
    neg_mid = jnp.where(is_lat, 0.0, NEG_INF)
    neg_right = jnp.where(has_right, 0.0, NEG_INF)
    bias_left = jnp.where(kc >= rq, neg_left, NEG_INF)
    bias_right = jnp.where(kc <= rq, neg_right, NEG_INF)

    def masked(s):
        band = s[:, ctx_len:]
        return jnp.concatenate([s[:, :ctx_len], band[:, :BLOCK] + bias_left, band[:, BLOCK:2 * BLOCK] + neg_mid,
                                band[:, 2 * BLOCK:] + bias_right], axis=1)

    heads = range(ATT_KV_HEADS)
    qh = [jnp.concatenate([q_head(h * ATT_GROUP + g) for g in range(ATT_GROUP)], axis=0).astype(BF16) for h in heads]
    s = [masked(lax.dot_general(qh[h], kcat[:, h * HEAD_DIM:(h + 1) * HEAD_DIM], (((1,), (1,)), ((), ())),
                                preferred_element_type=F32)) for h in heads]
    sk = []
    for h in heads:
        col = jnp.zeros((rows, 1), F32)
        for g in range(ATT_GROUP):
            col = jnp.where(grp == g, sink_ref[h * ATT_GROUP + g], col)
        sk.append(col)
    m = [jnp.maximum(jnp.max(s[h], axis=1, keepdims=True), sk[h]) for h in heads]
    p = [jnp.exp(s[h] - m[h]) for h in heads]
    den = [jnp.sum(p[h], axis=1, keepdims=True) + jnp.exp(sk[h] - m[h]) for h in heads]
    o = [jnp.dot(p[h].astype(BF16), vcat[:, h * HEAD_DIM:(h + 1) * HEAD_DIM], preferred_element_type=F32) / den[h]
         for h in heads]
    att = jnp.concatenate([o[h][g * BLOCK:(g + 1) * BLOCK] for h in heads for g in range(ATT_GROUP)], axis=1)
    o_ref[0] = (att * _silu(z_ref[0].astype(F32))).astype(o_ref.dtype)


def _attn_call(p, sink, cos, sin, ctx_len):
    B, T, _ = p.shape
    nb = T // BLOCK
    return pl.pallas_call(
        functools.partial(_attn_kernel, ctx_len=ctx_len, n_blocks=nb),
        grid=(B, nb),
        in_specs=[
            pl.BlockSpec(memory_space=pltpu.SMEM),
            pl.BlockSpec((1, BLOCK, ATT_WIDTH), lambda b, i: (b, i, OFF_AQ // ATT_WIDTH)),
            pl.BlockSpec((1, BLOCK, ATT_WIDTH), lambda b, i: (b, i, OFF_AZ // ATT_WIDTH)),
            pl.BlockSpec((1, T, LANES), lambda b, i: (b, 0, OFF_AK // LANES)),
            pl.BlockSpec((1, T, LANES), lambda b, i: (b, 0, OFF_AV // LANES)),
            pl.BlockSpec((BLOCK, LANES), lambda b, i: (i, 0)),
            pl.BlockSpec((BLOCK, LANES), lambda b, i: (i, 0)),
            pl.BlockSpec((T, LANES), lambda b, i: (0, 0)),
            pl.BlockSpec((T, LANES), lambda b, i: (0, 0)),
        ],
        out_specs=pl.BlockSpec((1, BLOCK, ATT_WIDTH), lambda b, i: (b, i, 0)),
        out_shape=jax.ShapeDtypeStruct((B, T, ATT_WIDTH), BF16),
        scratch_shapes=[pltpu.VMEM((T, LANES), BF16)],
        compiler_params=_params("arbitrary", "arbitrary"),
        name="attention",
    )(sink, p, p, p, p, cos, sin, cos, sin)


def _rope_tables(n, ctx_len):
    pos = jnp.arange(n)
    row = (pos // GRID_W).astype(F32)
    colp = (pos % GRID_W).astype(F32)
    inv = ROPE_BASE ** (-jnp.arange(ROPE_PAIRS, dtype=F32) / ROPE_PAIRS)
    lane = jnp.arange(LANES)
    axis = (lane % HEAD_DIM) // (2 * ROPE_PAIRS)
    pair = lane % ROPE_PAIRS
    ang = jnp.where(axis[None, :] == 0, row[:, None], colp[:, None]) * inv[pair][None, :]
    sign = jnp.where((lane % (2 * ROPE_PAIRS)) < ROPE_PAIRS, -1.0, 1.0).astype(F32)
    cos = jnp.concatenate([jnp.ones((ctx_len, LANES), F32), jnp.cos(ang)], axis=0)
    sin = jnp.concatenate([jnp.zeros((ctx_len, LANES), F32), jnp.sin(ang) * sign[None, :]], axis=0)
    return cos, sin


def _conv_kernel(a_ref, b_ref, z_ref, ap_ref, bp_ref, an_ref, bn_ref, w_ref, cb_ref, g_ref, nb_ref,
                 o_ref, h_scr, shift_scr, *, ctx_tiles, n_tiles):
    i = pl.program_id(1)

    def glu(a, b):
        return a.astype(F32) * jax.nn.sigmoid(b.astype(F32))

    has_prev = jnp.logical_and(i != 0, i != ctx_tiles)
    has_next = jnp.logical_and(i != ctx_tiles - 1, i != n_tiles - 1)
    h_scr[0:CONV_HALO, :] = jnp.where(has_prev, glu(ap_ref[0], bp_ref[0]), 0.0)
    h_scr[CONV_HALO:CONV_HALO + ROW_TILE, :] = glu(a_ref[0], b_ref[0])
    h_scr[CONV_HALO + ROW_TILE:, :] = jnp.where(has_next, glu(an_ref[0], bn_ref[0]), 0.0)

    base = CONV_HALO - CONV_TAPS // 2
    span = ROW_TILE + CONV_HALO + 8
    acc = jnp.zeros((ROW_TILE, CONV_WIDTH), F32)
    for r in range(8):
        if r:
            shift_scr[...] = h_scr[r:r + span, :]
        src = shift_scr if r else h_scr
        for off in range(r, base + CONV_TAPS, 8):
            k = off - base
            if k >= 0:
                acc = acc + w_ref[k:k + 1, :] * src[off - r:off - r + ROW_TILE, :]
    acc = acc + cb_ref[...]
    mu = jnp.mean(acc, axis=1, keepdims=True)
    cen = acc - mu
    var = jnp.mean(cen * cen, axis=1, keepdims=True)
    hn = cen * lax.rsqrt(var + LN_EPS) * g_ref[...] + nb_ref[...]
    o_ref[0] = (_silu(hn) * _silu(z_ref[0].astype(F32))).astype(o_ref.dtype)


def _conv_call(p, conv_w, conv_b, norm_g, norm_b, ctx_len):
    B, T, _ = p.shape
    nt = T // ROW_TILE
    hpt = ROW_TILE // CONV_HALO
    last_halo = T // CONV_HALO - 1
    ca, cb_, cz = OFF_BA // CONV_WIDTH, OFF_BB // CONV_WIDTH, OFF_BZ // CONV_WIDTH

    def prev_map(col):
        return lambda b, i: (b, jnp.maximum(i * hpt - 1, 0), col)

    def next_map(col):
        return lambda b, i: (b, jnp.minimum((i + 1) * hpt, last_halo), col)

    row = lambda v: v.reshape(1, CONV_WIDTH)
    vec_spec = pl.BlockSpec((1, CONV_WIDTH), lambda b, i: (0, 0))
    return pl.pallas_call(
        functools.partial(_conv_kernel, ctx_tiles=ctx_len // ROW_TILE, n_tiles=nt),
        grid=(B, nt),
        in_specs=[
            pl.BlockSpec((1, ROW_TILE, CONV_WIDTH), lambda b, i: (b, i, ca)),
            pl.BlockSpec((1, ROW_TILE, CONV_WIDTH), lambda b, i: (b, i, cb_)),
            pl.BlockSpec((1, ROW_TILE, CONV_WIDTH), lambda b, i: (b, i, cz)),
            pl.BlockSpec((1, CONV_HALO, CONV_WIDTH), prev_map(ca)),
            pl.BlockSpec((1, CONV_HALO, CONV_WIDTH), prev_map(cb_)),
            pl.BlockSpec((1, CONV_HALO, CONV_WIDTH), next_map(ca)),
            pl.BlockSpec((1, CONV_HALO, CONV_WIDTH), next_map(cb_)),
            pl.BlockSpec((CONV_TAPS, CONV_WIDTH), lambda b, i: (0, 0)),
            vec_spec, vec_spec, vec_spec,
        ],
        out_specs=pl.BlockSpec((1, ROW_TILE, CONV_WIDTH), lambda b, i: (b, i, 0)),
        out_shape=jax.ShapeDtypeStruct((B, T, CONV_WIDTH), BF16),
        scratch_shapes=[pltpu.VMEM((ROW_TILE + 2 * CONV_HALO, CONV_WIDTH), F32),
                        pltpu.VMEM((ROW_TILE + CONV_HALO + 8, CONV_WIDTH), F32)],
        compiler_params=_params("arbitrary", "arbitrary"),
        name="conformer_conv",
    )(p, p, p, p, p, p, p, conv_w, row(conv_b), row(norm_g), row(norm_b))


DN_PACK = 4
DN_GROUP = 6


def _split_bf16(t):
    hi = t.astype(BF16)
    return hi, (t - hi.astype(F32)).astype(BF16)


def _dot(a, b):
    return jnp.dot(a, b, preferred_element_type=F32)


def _dn_kernel(q_ref, k_ref, v_ref, z_ref, ab_ref, wq_ref, wk_ref, wv_ref, alog_ref, dtb_ref, gain_ref,
               o_ref, qs, ks, vs, kts, gb, bb, grow, u_scr, wq_scr, ka_scr, cd_scr, o_scr, *, ctx_len, total):
    h = pl.program_id(1)
    n_chunks = total // CHUNK
    n_pairs = n_chunks // 2
    ctx_chunks = ctx_len // CHUNK
    tr = ROW_TILE
    wide = DN_PACK * CHUNK
    lane = lax.broadcasted_iota(jnp.int32, (1, LANES), 1)
    rowi = lax.broadcasted_iota(jnp.int32, (tr, 1), 0)

    def short_conv(ref, w_ref, r0):
        x = ref[0, r0:r0 + tr, :].astype(F32)
        zero = jnp.zeros((1, LANES), F32)
        prev_row = zero if r0 in (0, ctx_len) else ref[0, r0 - 16:r0, :].astype(F32)[15:16]
        next_row = zero if r0 + tr in (ctx_len, total) else ref[0, r0 + tr:r0 + tr + 16, :].astype(F32)[0:1]
        xp = jnp.where(rowi == 0, prev_row, pltpu.roll(x, 1, 0))
        xn = jnp.where(rowi == tr - 1, next_row, pltpu.roll(x, tr - 1, 0))
        y = w_ref[0:1, :] * xp + w_ref[1:2, :] * x + w_ref[2:3, :] * xn
        return _silu(y)

    def l2norm(t):
        return t * lax.rsqrt(jnp.sum(t * t, axis=1, keepdims=True) + RMS_EPS)

    def lane_pick(t, idx):
        return jnp.broadcast_to(jnp.sum(jnp.where(lane == idx, t, 0.0), axis=1, keepdims=True), t.shape)

    for r0 in range(0, total, tr):
        qs[r0:r0 + tr, :] = l2norm(short_conv(q_ref, wq_ref, r0)) * (DN_HEAD_DIM ** -0.5)
        kn = l2norm(short_conv(k_ref, wk_ref, r0))
        ks[r0:r0 + tr, :] = kn
        kts[:, r0:r0 + tr] = kn.T
        vs[r0:r0 + tr, :] = short_conv(v_ref, wv_ref, r0)
        a = ab_ref[0, r0:r0 + tr, :]
        pre = a + dtb_ref[...]
        softplus = jnp.maximum(pre, 0.0) + jnp.log1p(jnp.exp(-jnp.abs(pre)))
        g_all = -jnp.exp(alog_ref[...]) * softplus
        b_all = jax.nn.sigmoid(a)
        for d in range(2):
            g_sel = lane_pick(g_all, d * DN_HEADS + h)
            gb[d, r0:r0 + tr, :] = g_sel
            grow[d, :, r0:r0 + tr] = g_sel.T[0:8, :]
            bb[d, r0:r0 + tr, :] = lane_pick(b_all, 2 * DN_HEADS + d * DN_HEADS + h)

    prow = lax.broadcasted_iota(jnp.int32, (CHUNK, wide), 0)
    plane = lax.broadcasted_iota(jnp.int32, (CHUNK, wide), 1)
    pcol = plane % CHUNK
    ahead = jnp.where(plane < 2 * CHUNK, prow - pcol, pcol - prow)
    incl = ahead >= 0
    strict = ahead > 0
    eye = jnp.where(prow == pcol, 1.0, 0.0).astype(F32)
    level_masks = tuple(jnp.logical_and(prow // (2 * s) == pcol // (2 * s), prow // s != pcol // s)
                        for s in (1, 2, 4, 8, 16, 32))
    slot_masks = tuple(jnp.where(plane // CHUNK == i, 1.0, 0.0).astype(BF16) for i in range(DN_PACK))
    ci = lax.broadcasted_iota(jnp.int32, (CHUNK, CHUNK), 0)
    cj = lax.broadcasted_iota(jnp.int32, (CHUNK, CHUNK), 1)
    lower_ones = jnp.where(cj <= ci, 1.0, 0.0).astype(BF16)
    wr = lax.broadcasted_iota(jnp.int32, (wide, wide), 0)
    wc = lax.broadcasted_iota(jnp.int32, (wide, wide), 1)
    same_slot = wr // CHUNK == wc // CHUNK
    along = jnp.where(wr < 2 * CHUNK, wc - wr, wr - wc) >= 0
    row_cumsum = jnp.where(jnp.logical_and(same_slot, along), 1.0, 0.0).astype(BF16)
    left_half = lax.broadcasted_iota(jnp.int32, (CHUNK, LANES), 1) < CHUNK
    left_half_row = lane < CHUNK

    def block_diag(y):
        return jnp.concatenate([y * m for m in slot_masks], axis=0)

    def pack(t0, t1, t2, t3, mask=left_half):
        return jnp.concatenate([jnp.where(mask, t0, t1), jnp.where(mask, t2, t3)], axis=1)

    def local(gi, carry):
        pairs = [gi * DN_GROUP + j for j in range(DN_GROUP)]
        stage = []
        for p in pairs:
            rows = [pl.ds(pl.multiple_of(p * 2 * CHUNK + i * CHUNK, CHUNK), CHUNK) for i in range(2)]
            cols = pl.ds(pl.multiple_of(p * LANES, LANES), LANES)
            k = [ks[r, :] for r in rows]
            q = [qs[r, :] for r in rows]
            v = [vs[r, :] for r in rows]
            g = [gb[d, r, :] for d in range(2) for r in rows]
            be = [bb[d, r, :] for d in range(2) for r in rows]
            g_hi, g_lo = _split_bf16(jnp.concatenate(g, axis=1))
            prefix = _dot(lower_ones, g_hi) + _dot(lower_ones, g_lo)
            gcol, g_last = [], []
            for i in range(DN_PACK):
                pf = prefix[:, i * LANES:(i + 1) * LANES]
                tot = pf[CHUNK - 1:CHUNK, :]
                gcol.append(pf if i < 2 else tot - pf + g[i])
                g_last.append(tot)
            r_hi, r_lo = _split_bf16(jnp.concatenate(
                [jnp.broadcast_to(grow[d, 0:1, cols], (8, LANES)) for d in range(2)], axis=1))
            grow_cum = (_dot(r_hi, row_cumsum) + _dot(r_lo, row_cumsum))[0:1, :]
            dm = jnp.exp(jnp.where(incl, pack(*gcol) - grow_cum, NEG_INF))
            lhs = jnp.concatenate([jnp.concatenate([k[i], q[i]], axis=0) for i in range(2)], axis=1).astype(BF16)
            zeros = jnp.zeros((CHUNK, LANES), F32)
            rhs = jnp.concatenate([jnp.concatenate([k[0], zeros], axis=1),
                                   jnp.concatenate([zeros, k[1]], axis=1)], axis=0).astype(BF16)
            kkqk = lax.dot_general(lhs, rhs, (((1,), (1,)), ((), ())), preferred_element_type=F32)
            kk = jnp.concatenate([kkqk[:CHUNK], kkqk[:CHUNK]], axis=1)
            qk = jnp.concatenate([kkqk[CHUNK:], kkqk[CHUNK:]], axis=1)
            a_mat = jnp.where(strict, kk * dm, 0.0) * pack(*be)
            aqk = jnp.where(incl, qk * dm, 0.0)
            a_levels = [block_diag(jnp.where(m, a_mat, 0.0).astype(BF16)) for m in level_masks[1:]]
            inv = eye - jnp.where(level_masks[0], a_mat, 0.0)
            stage.append(dict(rows=rows, cols=cols, k=k, q=q, v=v, be=be, gcol=gcol, g_last=g_last,
                              grow_cum=grow_cum, aqk=aqk, a_levels=a_levels, inv=inv))

        for lvl in range(len(level_masks) - 1):
            inv_b = [st["inv"].astype(BF16) for st in stage]
            t1 = [_dot(ib, st["a_levels"][lvl]) for ib, st in zip(inv_b, stage)]
            t2 = [_dot(t.astype(BF16), block_diag(ib)) for t, ib in zip(t1, inv_b)]
            for st, t in zip(stage, t2):
                st["inv"] = st["inv"] - t

        outs = []
        for st in stage:
            k, q, v, be, gcol, g_last = st["k"], st["q"], st["v"], st["be"], st["gcol"], st["g_last"]
            ek = [jnp.exp(t) for t in gcol]
            rstack = jnp.concatenate(
                [jnp.concatenate([v[i % 2] * be[i], k[i % 2] * be[i] * ek[i]], axis=1) for i in range(DN_PACK)],
                axis=0).astype(BF16)
            uw = _dot(block_diag(st["inv"].astype(BF16)), rstack)
            e_row = jnp.exp(pack(*g_last, mask=left_half_row) - st["grow_cum"])
            kt = kts[:, st["cols"]]
            outs.append(dict(
                u=[uw[i * CHUNK:(i + 1) * CHUNK, :LANES] for i in range(DN_PACK)],
                wq=[jnp.concatenate([uw[i * CHUNK:(i + 1) * CHUNK, LANES:], q[i % 2] * ek[i]], axis=0).astype(BF16)
                    for i in range(DN_PACK)],
                ka=[jnp.concatenate([kt * e_row[:, d * LANES:(d + 1) * LANES],
                                     st["aqk"][:, d * LANES:(d + 1) * LANES]], axis=0).astype(BF16)
                    for d in range(2)],
                cd=[jnp.broadcast_to(jnp.exp(t), (8, LANES)) for t in g_last]))

        for p, st, out in zip(pairs, stage, outs):
            for i in range(DN_PACK):
                d, j = i // 2, i % 2
                u_scr[d, st["rows"][j], :] = out["u"][i]
                wq_scr[d, 2 * p + j] = out["wq"][i]
                cd_scr[d, 2 * p + j] = out["cd"][i]
            for d in range(2):
                ka_scr[d, p] = out["ka"][d]
        return carry

    lax.fori_loop(0, n_pairs // DN_GROUP, local, 0)

    def step(t, states):
        new_states = []
        for d in range(2):
            if d == 0:
                c = t
            else:
                c = jnp.where(t < ctx_chunks, ctx_chunks - 1 - t, n_chunks - 1 + ctx_chunks - t)
            rows = pl.ds(pl.multiple_of(c * CHUNK, CHUNK), CHUNK)
            s = states[d]
            ws = _dot(wq_scr[d, c], s.astype(BF16))
            v_new = u_scr[d, rows, :] - ws[:CHUNK]
            odd = (c % 2).astype(F32)
            v_rows = jnp.concatenate([v_new * (1.0 - odd), v_new * odd], axis=0).astype(BF16)
            ka = _dot(ka_scr[d, c // 2], v_rows)
            o_scr[d, rows, :] = ws[CHUNK:] + ka[LANES:]
            new_states.append(s * cd_scr[d, c][0:1, :] + ka[:LANES])
        return tuple(new_states)

    zero_state = jnp.zeros((DN_HEAD_DIM, DN_HEAD_DIM), F32)
    lax.fori_loop(0, n_chunks, step, (zero_state, zero_state))

    for r0 in range(0, total, tr):
        o = o_scr[0, r0:r0 + tr, :] + o_scr[1, r0:r0 + tr, :]
        o = o * lax.rsqrt(jnp.mean(o * o, axis=1, keepdims=True) + RMS_EPS) * gain_ref[...]
        o_ref[0, r0:r0 + tr, :] = (o * _silu(z_ref[0, r0:r0 + tr, :].astype(F32))).astype(o_ref.dtype)


def _dn_call(p, ab, conv_w, a_log, dt_bias, gain, ctx_len):
    B, T, _ = p.shape
    nc = T // CHUNK
    assert nc % (2 * DN_GROUP) == 0

    def pad_row(v):
        return jnp.zeros((1, LANES), F32).at[0, :v.size].set(v.reshape(-1))

    def col(off):
        return lambda b, h: (b, 0, off // LANES + h)

    seq = lambda off: pl.BlockSpec((1, T, LANES), col(off))
    tap = lambda base: pl.BlockSpec((3, LANES), lambda b, h: (0, base + h))
    vec = pl.BlockSpec((1, LANES), lambda b, h: (0, 0))
    return pl.pallas_call(
        functools.partial(_dn_kernel, ctx_len=ctx_len, total=T),
        grid=(B, DN_HEADS),
        in_specs=[
            seq(OFF_CQ), seq(OFF_CK), seq(OFF_CV), seq(OFF_CZ),
            pl.BlockSpec((1, T, LANES), lambda b, h: (b, 0, 0)),
            tap(0), tap(DN_HEADS), tap(2 * DN_HEADS),
            vec, vec, vec,
        ],
        out_specs=pl.BlockSpec((1, T, LANES), lambda b, h: (b, 0, h)),
        out_shape=jax.ShapeDtypeStruct((B, T, DN_WIDTH), BF16),
        scratch_shapes=[
            pltpu.VMEM((T, LANES), F32), pltpu.VMEM((T, LANES), F32), pltpu.VMEM((T, LANES), F32),
            pltpu.VMEM((LANES, T), F32),
            pltpu.VMEM((2, T, LANES), F32), pltpu.VMEM((2, T, LANES), F32),
            pltpu.VMEM((2, 8, T), F32),
            pltpu.VMEM((2, T, LANES), F32),
            pltpu.VMEM((2, nc, 2 * CHUNK, LANES), BF16),
            pltpu.VMEM((2, nc // 2, LANES + CHUNK, LANES), BF16),
            pltpu.VMEM((2, nc, 8, LANES), F32),
            pltpu.VMEM((2, T, LANES), F32),
        ],
        compiler_params=_params("arbitrary", "arbitrary"),
        name="gated_deltanet",
    )(p, p, p, p, ab, conv_w, conv_w, conv_w, pad_row(a_log), pad_row(dt_bias), gain.reshape(1, LANES))


def _merge_kernel(ya_ref, yb_ref, yc_ref, g0_ref, g1_ref, g2_ref, x_ref, mb_ref, mc_ref, wb_ref, wo_ref,
                  lg_ref, lb_ref, o_ref, *, ctx_tiles, first_tile):
    i = pl.program_id(1) + first_tile

    def branch(y_ref, g_ref, r):
        proj = jnp.dot(y_ref[0], wb_ref[r], preferred_element_type=F32)
        return jax.nn.sigmoid(g_ref[0].astype(F32)) * proj

    merged = branch(ya_ref, g0_ref, 0) + branch(yb_ref, g1_ref, 1) + branch(yc_ref, g2_ref, 2)
    out = jnp.dot(merged.astype(BF16), wo_ref[...], preferred_element_type=F32)
    gate = jnp.where(i < ctx_tiles, mc_ref[0, 2:3, :], mb_ref[0, 2:3, :])
    t = DEEPNORM_ALPHA * x_ref[0] + gate * out
    mu = jnp.mean(t, axis=1, keepdims=True)
    cen = t - mu
    var = jnp.mean(cen * cen, axis=1, keepdims=True)
    o_ref[0] = cen * lax.rsqrt(var + LN_EPS) * lg_ref[...] + lb_ref[...]


def _merge_call(ya, yb, yc, p, xc, mods, w_branch, w_out, ln_g, ln_b, ctx_len, latent_only):
    B, T, _ = xc.shape
    ctx_row = B
    gcol = OFF_GATE // D_MODEL
    ctx_tiles = ctx_len // ROW_TILE
    first = ctx_tiles if latent_only else 0
    n_tiles = T // ROW_TILE - first
    y_spec = pl.BlockSpec((1, ROW_TILE, ATT_WIDTH), lambda b, i: (b, i + first, 0))
    gate_spec = lambda r: pl.BlockSpec((1, ROW_TILE, D_MODEL), lambda b, i: (b, i + first, gcol + r))
    vec = pl.BlockSpec((1, D_MODEL), lambda b, i: (0, 0))
    return pl.pallas_call(
        functools.partial(_merge_kernel, ctx_tiles=ctx_tiles, first_tile=first),
        grid=(B, n_tiles),
        in_specs=[
            y_spec, y_spec, y_spec,
            gate_spec(0), gate_spec(1), gate_spec(2),
            pl.BlockSpec((1, ROW_TILE, D_MODEL), lambda b, i: (b, i + first, 0)),
            pl.BlockSpec((1, 3, D_MODEL), lambda b, i: (b, 0, 0)),
            pl.BlockSpec((1, 3, D_MODEL), lambda b, i: (ctx_row, 0, 0)),
            pl.BlockSpec((N_BRANCH, ATT_WIDTH, D_MODEL), lambda b, i: (0, 0, 0)),
            pl.BlockSpec((D_MODEL, D_MODEL), lambda b, i: (0, 0)),
            vec, vec,
        ],
        out_specs=pl.BlockSpec((1, ROW_TILE, D_MODEL), lambda b, i: (b, i, 0)),
        out_shape=jax.ShapeDtypeStruct((B, n_tiles * ROW_TILE, D_MODEL), F32),
        compiler_params=_params("arbitrary", "arbitrary"),
        name="merge",
    )(ya, yb, yc, p, p, p, xc, mods, mods, w_branch, w_out, ln_g.reshape(1, D_MODEL), ln_b.reshape(1, D_MODEL))


def _permute_in_proj(w_in):
    pieces = [
        w_in[..., 0:512], w_in[..., 768:1280],
        w_in[..., 1280:2304], w_in[..., 2304:2816],
        w_in[..., 4352:4864],
        w_in[..., 4880:7952],
        w_in[..., 2816:4352],
        w_in[..., 512:640], w_in[..., 640:768],
        w_in[..., 4864:4880],
    ]
    pad = IN_WIDTH_PAD - sum(piece.shape[-1] for piece in pieces)
    pieces.append(jnp.zeros(w_in.shape[:-1] + (pad,), w_in.dtype))
    return jnp.concatenate(pieces, axis=-1).astype(BF16)


def kernel(x, c, ctx, c_ctx, w_ada, b_ada, w_in, a_sink, b_conv_w, b_conv_b, b_norm_g, b_norm_b,
           c_conv_w, c_a_log, c_dt_bias, c_norm_g, w_branch, w_out, ln_g, ln_b):
    B, n, _ = x.shape
    ctx_len = ctx.shape[1]
    assert ctx_len % ROW_TILE == 0 and n % ROW_TILE == 0 and n % GRID_W == 0

    xc = jnp.concatenate([ctx, x], axis=1)
    mod_rows = -(-(B + 1) // 8) * 8
    cc = jnp.concatenate([c, c_ctx[None, :], jnp.zeros((mod_rows - B - 1, D_MODEL), F32)], axis=0)
    mods_all = _ada_call(cc, w_ada, b_ada).reshape(DEPTH, mod_rows, 3, D_MODEL)
    w_perm = _permute_in_proj(w_in)
    wb16 = w_branch.astype(BF16)
    wo16 = w_out.astype(BF16)
    cos, sin = _rope_tables(n, ctx_len)

    for l in range(DEPTH):
        mods = mods_all[l]
        p, ab = _inproj_call(xc, mods, w_perm[l], ctx_len, BF16)
        ya = _attn_call(p, a_sink[l], cos, sin, ctx_len)
        yb = _conv_call(p, b_conv_w[l], b_conv_b[l], b_norm_g[l], b_norm_b[l], ctx_len)
        yc = _dn_call(p, ab, c_conv_w[l], c_a_log[l], c_dt_bias[l], c_norm_g[l], ctx_len)
        xc = _merge_call(ya, yb, yc, p, xc, mods, wb16[l], wo16[l], ln_g[l], ln_b[l], ctx_len,
                         latent_only=l == DEPTH - 1)
    return xc
```

```python
import functools

import jax
import jax.numpy as jnp
from jax import lax
from jax.experimental import pallas as pl
from jax.experimental.pallas import tpu as pltpu

F32 = jnp.float32
BF16 = jnp.bfloat16
HIGHEST = lax.Precision.HIGHEST

LANES = 128
VMEM_LIMIT = 56 * 1024 * 1024

D_MODEL = 1024
DEPTH = 4
GRID_W = 64
HEAD_DIM = 64
ATT_Q_HEADS = 8
ATT_KV_HEADS = 2
ATT_GROUP = ATT_Q_HEADS // ATT_KV_HEADS
ATT_WIDTH = 512
BLOCK = 128
ROPE_BASE = 10000.0
ROPE_PAIRS = HEAD_DIM // 4
CONV_WIDTH = 512
CONV_TAPS = 31
CONV_HALO = 16
DN_HEADS = 4
DN_HEAD_DIM = 128
DN_WIDTH = 512
CHUNK = 64
N_BRANCH = 3
DEEPNORM_ALPHA = (2 * DEPTH) ** 0.25
LN_EPS = 1e-5
RMS_EPS = 1e-6
NEG_INF = -1e30

OFF_AQ, OFF_AZ = 0, 512
OFF_BA, OFF_BB, OFF_BZ = 1024, 1536, 2048
OFF_CZ = 2560
OFF_GATE = 3072
OFF_CQ, OFF_CK, OFF_CV = 6144, 6656, 7168
OFF_AK, OFF_AV = 7680, 7808
OFF_AB = 7936
IN_WIDTH_PAD = 8192
IN_TILE_N = 1024
ROW_TILE = 256


def _silu(t):
    return t * jax.nn.sigmoid(t)


def _params(*sem):
    return pltpu.CompilerParams(dimension_semantics=sem, vmem_limit_bytes=VMEM_LIMIT)


def _ada_kernel(cc_ref, w_ref, b_ref, o_ref):
    s = _silu(cc_ref[...])
    o_ref[0] = jnp.dot(s, w_ref[0], precision=HIGHEST, preferred_element_type=F32) + b_ref[0]


def _ada_call(cc, w_ada, b_ada):
    rows = cc.shape[0]
    return pl.pallas_call(
        _ada_kernel,
        grid=(DEPTH, 3),
        in_specs=[
            pl.BlockSpec((rows, D_MODEL), lambda l, j: (0, 0)),
            pl.BlockSpec((1, D_MODEL, D_MODEL), lambda l, j: (l, 0, j)),
            pl.BlockSpec((1, 1, D_MODEL), lambda l, j: (l, 0, j)),
        ],
        out_specs=pl.BlockSpec((1, rows, D_MODEL), lambda l, j: (l, 0, j)),
        out_shape=jax.ShapeDtypeStruct((DEPTH, rows, 3 * D_MODEL), F32),
        compiler_params=_params("arbitrary", "arbitrary"),
        name="ada_mod",
    )(cc, w_ada, b_ada.reshape(DEPTH, 1, 3 * D_MODEL))


def _inproj_kernel(x_ref, mb_ref, mc_ref, w_ref, p_ref, ab_ref, u_scr, *, tm, ctx_len):
    i = pl.program_id(1)
    j = pl.program_id(2)

    @pl.when(j == 0)
    def _():
        row = lax.broadcasted_iota(jnp.int32, (tm, 1), 0) + i * tm
        is_ctx = row < ctx_len
        shift = jnp.where(is_ctx, mc_ref[0, 0:1, :], mb_ref[0, 0:1, :])
        scale = jnp.where(is_ctx, mc_ref[0, 1:2, :], mb_ref[0, 1:2, :])
        u_scr[...] = (x_ref[0] * (1.0 + scale) + shift).astype(BF16)

    p = jnp.dot(u_scr[...], w_ref[...], preferred_element_type=F32)
    p_ref[0] = p.astype(p_ref.dtype)

    @pl.when(j == OFF_AB // IN_TILE_N)
    def _():
        ab_ref[0] = p[:, OFF_AB % IN_TILE_N:OFF_AB % IN_TILE_N + LANES]


def _inproj_call(xc, mods, w, ctx_len, p_dtype):
    B, T, _ = xc.shape
    tm = T // 2
    nj = IN_WIDTH_PAD // IN_TILE_N
    ctx_row = B
    return pl.pallas_call(
        functools.partial(_inproj_kernel, tm=tm, ctx_len=ctx_len),
        grid=(B, T // tm, nj),
        in_specs=[
            pl.BlockSpec((1, tm, D_MODEL), lambda b, i, j: (b, i, 0)),
            pl.BlockSpec((1, 3, D_MODEL), lambda b, i, j: (b, 0, 0)),
            pl.BlockSpec((1, 3, D_MODEL), lambda b, i, j: (ctx_row, 0, 0)),
            pl.BlockSpec((D_MODEL, IN_TILE_N), lambda b, i, j: (0, j)),
        ],
        out_specs=[
            pl.BlockSpec((1, tm, IN_TILE_N), lambda b, i, j: (b, i, j)),
            pl.BlockSpec((1, tm, LANES), lambda b, i, j: (b, i, 0)),
        ],
        out_shape=[
            jax.ShapeDtypeStruct((B, T, IN_WIDTH_PAD), p_dtype),
            jax.ShapeDtypeStruct((B, T, LANES), F32),
        ],
        scratch_shapes=[pltpu.VMEM((tm, D_MODEL), BF16)],
        compiler_params=_params("arbitrary", "arbitrary", "arbitrary"),
        name="in_proj",
    )(xc, mods, mods, w)


def _rope(t, cos, sin_signed):
    lane = lax.broadcasted_iota(jnp.int32, (1, LANES), 1)
    partner = jnp.where((lane % 32) < 16, pltpu.roll(t, LANES - 16, 1), pltpu.roll(t, 16, 1))
    return t * cos + partner * sin_signed


def _attn_kernel(sink_ref, q_ref, z_ref, k_ref, v_ref, cq_ref, sq_ref, ck_ref, sk_ref, o_ref, kr_scr,
                 *, ctx_len, n_blocks):
    i = pl.program_id(1)
    ctx_blocks = ctx_len // BLOCK

    @pl.when(i == 0)
    def _():
        kr_scr[...] = _rope(k_ref[0].astype(F32), ck_ref[...], sk_ref[...]).astype(BF16)

    q = q_ref[0].astype(F32) * (HEAD_DIM ** -0.5)
    cq = cq_ref[...]
    sq = sq_ref[...]
    q_pairs = [_rope(q[:, LANES * t:LANES * (t + 1)], cq, sq) for t in range(ATT_WIDTH // LANES)]

    def q_head(hq):
        off = (hq % 2) * HEAD_DIM
        return q_pairs[hq // 2][:, off:off + HEAD_DIM]

    left = pl.multiple_of(jnp.maximum(i - 1, 0) * BLOCK, BLOCK)
    mid = pl.multiple_of(i * BLOCK, BLOCK)
    right = pl.multiple_of(jnp.minimum(i + 1, n_blocks - 1) * BLOCK, BLOCK)
    kcat = jnp.concatenate([kr_scr[0:ctx_len, :], kr_scr[pl.ds(left, BLOCK), :],
                            kr_scr[pl.ds(mid, BLOCK), :], kr_scr[pl.ds(right, BLOCK), :]], axis=0)
    vcat = jnp.concatenate([v_ref[0, 0:ctx_len, :], v_ref[0, pl.ds(left, BLOCK), :],
                            v_ref[0, pl.ds(mid, BLOCK), :], v_ref[0, pl.ds(right, BLOCK), :]],
                           axis=0).astype(BF16)

    rows = ATT_GROUP * BLOCK
    is_lat = i >= ctx_blocks
    has_left = i >= ctx_blocks + 1
    has_right = jnp.logical_and(is_lat, i <= n_blocks - 2)
    rq = lax.broadcasted_iota(jnp.int32, (rows, BLOCK), 0) % BLOCK
    kc = lax.broadcasted_iota(jnp.int32, (rows, BLOCK), 1)
    grp = lax.broadcasted_iota(jnp.int32, (rows, 1), 0) // BLOCK
    neg_left = jnp.where(has_left, 0.0, NEG_INF)
    neg_mid = jnp.where(is_lat, 0.0, NEG_INF)
    neg_right = jnp.where(has_right, 0.0, NEG_INF)
    bias_left = jnp.where(kc >= rq, neg_left, NEG_INF)
    bias_right = jnp.where(kc <= rq, neg_right, NEG_INF)

    def masked(s):
        band = s[:, ctx_len:]
        return jnp.concatenate([s[:, :ctx_len], band[:, :BLOCK] + bias_left, band[:, BLOCK:2 * BLOCK] + neg_mid,
                                band[:, 2 * BLOCK:] + bias_right], axis=1)

    heads = range(ATT_KV_HEADS)
    qh = [jnp.concatenate([q_head(h * ATT_GROUP + g) for g in range(ATT_GROUP)], axis=0).astype(BF16) for h in heads]
    s = [masked(lax.dot_general(qh[h], kcat[:, h * HEAD_DIM:(h + 1) * HEAD_DIM], (((1,), (1,)), ((), ())),
                                preferred_element_type=F32)) for h in heads]
    sk = []
    for h in heads:
        col = jnp.zeros((rows, 1), F32)
        for g in range(ATT_GROUP):
            col = jnp.where(grp == g, sink_ref[h * ATT_GROUP + g], col)
        sk.append(col)
    m = [jnp.maximum(jnp.max(s[h], axis=1, keepdims=True), sk[h]) for h in heads]
    p = [jnp.exp(s[h] - m[h]) for h in heads]
    den = [jnp.sum(p[h], axis=1, keepdims=True) + jnp.exp(sk[h] - m[h]) for h in heads]
    o = [jnp.dot(p[h].astype(BF16), vcat[:, h * HEAD_DIM:(h + 1) * HEAD_DIM], preferred_element_type=F32) / den[h]
         for h in heads]
    att = jnp.concatenate([o[h][g * BLOCK:(g + 1) * BLOCK] for h in heads for g in range(ATT_GROUP)], axis=1)
    o_ref[0] = (att * _silu(z_ref[0].astype(F32))).astype(o_ref.dtype)


def _attn_call(p, sink, cos, sin, ctx_len):
    B, T, _ = p.shape
    nb = T // BLOCK
    return pl.pallas_call(
        functools.partial(_attn_kernel, ctx_len=ctx_len, n_blocks=nb),
        grid=(B, nb),
        in_specs=[
            pl.BlockSpec(memory_space=pltpu.SMEM),
            pl.BlockSpec((1, BLOCK, ATT_WIDTH), lambda b, i: (b, i, OFF_AQ // ATT_WIDTH)),
            pl.BlockSpec((1, BLOCK, ATT_WIDTH), lambda b, i: (b, i, OFF_AZ // ATT_WIDTH)),
            pl.BlockSpec((1, T, LANES), lambda b, i: (b, 0, OFF_AK // LANES)),
            pl.BlockSpec((1, T, LANES), lambda b, i: (b, 0, OFF_AV // LANES)),
            pl.BlockSpec((BLOCK, LANES), lambda b, i: (i, 0)),
            pl.BlockSpec((BLOCK, LANES), lambda b, i: (i, 0)),
            pl.BlockSpec((T, LANES), lambda b, i: (0, 0)),
            pl.BlockSpec((T, LANES), lambda b, i: (0, 0)),
        ],
        out_specs=pl.BlockSpec((1, BLOCK, ATT_WIDTH), lambda b, i: (b, i, 0)),
        out_shape=jax.ShapeDtypeStruct((B, T, ATT_WIDTH), BF16),
        scratch_shapes=[pltpu.VMEM((T, LANES), BF16)],
        compiler_params=_params("arbitrary", "arbitrary"),
        name="attention",
    )(sink, p, p, p, p, cos, sin, cos, sin)


def _rope_tables(n, ctx_len):
    pos = jnp.arange(n)
    row = (pos // GRID_W).astype(F32)
    colp = (pos % GRID_W).astype(F32)
    inv = ROPE_BASE ** (-jnp.arange(ROPE_PAIRS, dtype=F32) / ROPE_PAIRS)
    lane = jnp.arange(LANES)
    axis = (lane % HEAD_DIM) // (2 * ROPE_PAIRS)
    pair = lane % ROPE_PAIRS
    ang = jnp.where(axis[None, :] == 0, row[:, None], colp[:, None]) * inv[pair][None, :]
    sign = jnp.where((lane % (2 * ROPE_PAIRS)) < ROPE_PAIRS, -1.0, 1.0).astype(F32)
    cos = jnp.concatenate([jnp.ones((ctx_len, LANES), F32), jnp.cos(ang)], axis=0)
    sin = jnp.concatenate([jnp.zeros((ctx_len, LANES), F32), jnp.sin(ang) * sign[None, :]], axis=0)
    return cos, sin


def _conv_kernel(a_ref, b_ref, z_ref, ap_ref, bp_ref, an_ref, bn_ref, w_ref, cb_ref, g_ref, nb_ref,
                 o_ref, h_scr, shift_scr, *, ctx_tiles, n_tiles):
    i = pl.program_id(1)

    def glu(a, b):
        return a.astype(F32) * jax.nn.sigmoid(b.astype(F32))

    has_prev = jnp.logical_and(i != 0, i != ctx_tiles)
    has_next = jnp.logical_and(i != ctx_tiles - 1, i != n_tiles - 1)
    h_scr[0:CONV_HALO, :] = jnp.where(has_prev, glu(ap_ref[0], bp_ref[0]), 0.0)
    h_scr[CONV_HALO:CONV_HALO + ROW_TILE, :] = glu(a_ref[0], b_ref[0])
    h_scr[CONV_HALO + ROW_TILE:, :] = jnp.where(has_next, glu(an_ref[0], bn_ref[0]), 0.0)

    base = CONV_HALO - CONV_TAPS // 2
    span = ROW_TILE + CONV_HALO + 8
    acc = jnp.zeros((ROW_TILE, CONV_WIDTH), F32)
    for r in range(8):
        if r:
            shift_scr[...] = h_scr[r:r + span, :]
        src = shift_scr if r else h_scr
        for off in range(r, base + CONV_TAPS, 8):
            k = off - base
            if k >= 0:
                acc = acc + w_ref[k:k + 1, :] * src[off - r:off - r + ROW_TILE, :]
    acc = acc + cb_ref[...]
    mu = jnp.mean(acc, axis=1, keepdims=True)
    cen = acc - mu
    var = jnp.mean(cen * cen, axis=1, keepdims=True)
    hn = cen * lax.rsqrt(var + LN_EPS) * g_ref[...] + nb_ref[...]
    o_ref[0] = (_silu(hn) * _silu(z_ref[0].astype(F32))).astype(o_ref.dtype)


def _conv_call(p, conv_w, conv_b, norm_g, norm_b, ctx_len):
    B, T, _ = p.shape
    nt = T // ROW_TILE
    hpt = ROW_TILE // CONV_HALO
    last_halo = T // CONV_HALO - 1
    ca, cb_, cz = OFF_BA // CONV_WIDTH, OFF_BB // CONV_WIDTH, OFF_BZ // CONV_WIDTH

    def prev_map(col):
        return lambda b, i: (b, jnp.maximum(i * hpt - 1, 0), col)

    def next_map(col):
        return lambda b, i: (b, jnp.minimum((i + 1) * hpt, last_halo), col)

    row = lambda v: v.reshape(1, CONV_WIDTH)
    vec_spec = pl.BlockSpec((1, CONV_WIDTH), lambda b, i: (0, 0))
    return pl.pallas_call(
        functools.partial(_conv_kernel, ctx_tiles=ctx_len // ROW_TILE, n_tiles=nt),
        grid=(B, nt),
        in_specs=[
            pl.BlockSpec((1, ROW_TILE, CONV_WIDTH), lambda b, i: (b, i, ca)),
            pl.BlockSpec((1, ROW_TILE, CONV_WIDTH), lambda b, i: (b, i, cb_)),
            pl.BlockSpec((1, ROW_TILE, CONV_WIDTH), lambda b, i: (b, i, cz)),
            pl.BlockSpec((1, CONV_HALO, CONV_WIDTH), prev_map(ca)),
            pl.BlockSpec((1, CONV_HALO, CONV_WIDTH), prev_map(cb_)),
            pl.BlockSpec((1, CONV_HALO, CONV_WIDTH), next_map(ca)),
            pl.BlockSpec((1, CONV_HALO, CONV_WIDTH), next_map(cb_)),
            pl.BlockSpec((CONV_TAPS, CONV_WIDTH), lambda b, i: (0, 0)),
            vec_spec, vec_spec, vec_spec,
        ],
        out_specs=pl.BlockSpec((1, ROW_TILE, CONV_WIDTH), lambda b, i: (b, i, 0)),
        out_shape=jax.ShapeDtypeStruct((B, T, CONV_WIDTH), BF16),
        scratch_shapes=[pltpu.VMEM((ROW_TILE + 2 * CONV_HALO, CONV_WIDTH), F32),
                        pltpu.VMEM((ROW_TILE + CONV_HALO + 8, CONV_WIDTH), F32)],
        compiler_params=_params("arbitrary", "arbitrary"),
        name="conformer_conv",
    )(p, p, p, p, p, p, p, conv_w, row(conv_b), row(norm_g), row(norm_b))


DN_PACK = 4
DN_GROUP = 3
DN_HEADS_PER_STEP = 2


def _split_bf16(t):
    hi = t.astype(BF16)
    return hi, (t - hi.astype(F32)).astype(BF16)


def _dot(a, b):
    return jnp.dot(a, b, preferred_element_type=F32)


def _dn_kernel(q_ref, k_ref, v_ref, z_ref, ab_ref, wq_ref, wk_ref, wv_ref, alog_ref, dtb_ref, gain_ref,
               o_ref, qs, ks, vs, kts, gb, bb, grow, o_scr, *, ctx_len, total):
    head0 = pl.program_id(1) * DN_HEADS_PER_STEP
    n_pairs = total // (2 * CHUNK)
    ctx_pairs = ctx_len // (2 * CHUNK)
    n_stages = n_pairs // DN_GROUP
    tr = ROW_TILE
    wide = DN_PACK * CHUNK
    heads = range(DN_HEADS_PER_STEP)
    lane = lax.broadcasted_iota(jnp.int32, (1, LANES), 1)
    rowi = lax.broadcasted_iota(jnp.int32, (tr, 1), 0)

    def short_conv(ref, w_ref, r0, hh):
        cols = slice(hh * LANES, (hh + 1) * LANES)
        x = ref[0, r0:r0 + tr, cols].astype(F32)
        zero = jnp.zeros((1, LANES), F32)
        prev_row = zero if r0 in (0, ctx_len) else ref[0, r0 - 16:r0, cols].astype(F32)[15:16]
        next_row = zero if r0 + tr in (ctx_len, total) else ref[0, r0 + tr:r0 + tr + 16, cols].astype(F32)[0:1]
        xp = jnp.where(rowi == 0, prev_row, pltpu.roll(x, 1, 0))
        xn = jnp.where(rowi == tr - 1, next_row, pltpu.roll(x, tr - 1, 0))
        y = w_ref[0:1, cols] * xp + w_ref[1:2, cols] * x + w_ref[2:3, cols] * xn
        return _silu(y)

    def l2norm(t):
        return t * lax.rsqrt(jnp.sum(t * t, axis=1, keepdims=True) + RMS_EPS)

    def lane_pick(t, idx):
        return jnp.broadcast_to(jnp.sum(jnp.where(lane == idx, t, 0.0), axis=1, keepdims=True), t.shape)

    for r0 in range(0, total, tr):
        a = ab_ref[0, r0:r0 + tr, :]
        pre = a + dtb_ref[...]
        softplus = jnp.maximum(pre, 0.0) + jnp.log1p(jnp.exp(-jnp.abs(pre)))
        g_all = -jnp.exp(alog_ref[...]) * softplus
        b_all = jax.nn.sigmoid(a)
        for hh in heads:
            qs[hh, r0:r0 + tr, :] = l2norm(short_conv(q_ref, wq_ref, r0, hh)) * (DN_HEAD_DIM ** -0.5)
            kn = l2norm(short_conv(k_ref, wk_ref, r0, hh))
            ks[hh, r0:r0 + tr, :] = kn
            kts[hh, :, r0:r0 + tr] = kn.T
            vs[hh, r0:r0 + tr, :] = short_conv(v_ref, wv_ref, r0, hh)
            for d in range(2):
                g_sel = lane_pick(g_all, d * DN_HEADS + head0 + hh)
                gb[hh, d, r0:r0 + tr, :] = g_sel
                grow[hh, d, :, r0:r0 + tr] = g_sel.T[0:8, :]
                bb[hh, d, r0:r0 + tr, :] = lane_pick(b_all, 2 * DN_HEADS + d * DN_HEADS + head0 + hh)

    prow = lax.broadcasted_iota(jnp.int32, (CHUNK, wide), 0)
    plane = lax.broadcasted_iota(jnp.int32, (CHUNK, wide), 1)
    pcol = plane % CHUNK
    ahead = jnp.where(plane < 2 * CHUNK, prow - pcol, pcol - prow)
    incl = ahead >= 0
    strict = ahead > 0
    eye = jnp.where(prow == pcol, 1.0, 0.0).astype(F32)
    level_masks = tuple(jnp.logical_and(prow // (2 * s) == pcol // (2 * s), prow // s != pcol // s)
                        for s in (1, 2, 4, 8, 16, 32))
    slot_masks = tuple(jnp.where(plane // CHUNK == i, 1.0, 0.0).astype(BF16) for i in range(DN_PACK))
    ci = lax.broadcasted_iota(jnp.int32, (CHUNK, CHUNK), 0)
    cj = lax.broadcasted_iota(jnp.int32, (CHUNK, CHUNK), 1)
    lower_ones = jnp.where(cj <= ci, 1.0, 0.0).astype(BF16)
    wr = lax.broadcasted_iota(jnp.int32, (wide, wide), 0)
    wc = lax.broadcasted_iota(jnp.int32, (wide, wide), 1)
    same_slot = wr // CHUNK == wc // CHUNK
    along = jnp.where(wr < 2 * CHUNK, wc - wr, wr - wc) >= 0
    row_cumsum = jnp.where(jnp.logical_and(same_slot, along), 1.0, 0.0).astype(BF16)
    left_half = lax.broadcasted_iota(jnp.int32, (CHUNK, LANES), 1) < CHUNK
    left_half_row = lane < CHUNK
    zeros_tile = jnp.zeros((CHUNK, LANES), F32)

    def block_diag(y):
        return jnp.concatenate([y * m for m in slot_masks], axis=0)

    def pack(t0, t1, t2, t3, mask=left_half):
        return jnp.concatenate([jnp.where(mask, t0, t1), jnp.where(mask, t2, t3)], axis=1)

    def scan_pairs(step):
        return step, jnp.where(step < ctx_pairs, ctx_pairs - 1 - step, n_pairs - 1 + ctx_pairs - step)

    def chunk_rows(chunk):
        return pl.ds(pl.multiple_of(chunk * CHUNK, CHUNK), CHUNK)

    def pair_cols(pair):
        return pl.ds(pl.multiple_of(pair * LANES, LANES), LANES)

    def local_stages(stage_idx, out):
        units = []
        for hh in heads:
            for j in range(DN_GROUP):
                pairs = scan_pairs(stage_idx * DN_GROUP + j)
                rows = [chunk_rows(2 * pairs[d] + i) for d in range(2) for i in range(2)]
                k = [ks[hh, r, :] for r in rows]
                q = [qs[hh, r, :] for r in rows]
                v = [vs[hh, r, :] for r in rows]
                g = [gb[hh, i // 2, rows[i], :] for i in range(DN_PACK)]
                be = [bb[hh, i // 2, rows[i], :] for i in range(DN_PACK)]
                g_hi, g_lo = _split_bf16(jnp.concatenate(g, axis=1))
                prefix = _dot(lower_ones, g_hi) + _dot(lower_ones, g_lo)
                gcol, g_last = [], []
                for i in range(DN_PACK):
                    pf = prefix[:, i * LANES:(i + 1) * LANES]
                    tot = pf[CHUNK - 1:CHUNK, :]
                    gcol.append(pf if i < 2 else tot - pf + g[i])
                    g_last.append(tot)
                r_hi, r_lo = _split_bf16(jnp.concatenate(
                    [jnp.broadcast_to(grow[hh, d, 0:1, pair_cols(pairs[d])], (8, LANES)) for d in range(2)], axis=1))
                grow_cum = (_dot(r_hi, row_cumsum) + _dot(r_lo, row_cumsum))[0:1, :]
                dm = jnp.exp(jnp.where(incl, pack(*gcol) - grow_cum, NEG_INF))
                kkqk = []
                for d in range(2):
                    k0, k1, q0, q1 = k[2 * d], k[2 * d + 1], q[2 * d], q[2 * d + 1]
                    lhs = jnp.concatenate([jnp.concatenate([k0, q0], axis=0),
                                           jnp.concatenate([k1, q1], axis=0)], axis=1).astype(BF16)
                    rhs = jnp.concatenate([jnp.concatenate([k0, zeros_tile], axis=1),
                                           jnp.concatenate([zeros_tile, k1], axis=1)], axis=0).astype(BF16)
                    kkqk.append(lax.dot_general(lhs, rhs, (((1,), (1,)), ((), ())), preferred_element_type=F32))
                kk = jnp.concatenate([t[:CHUNK] for t in kkqk], axis=1)
                qk = jnp.concatenate([t[CHUNK:] for t in kkqk], axis=1)
                a_mat = jnp.where(strict, kk * dm, 0.0) * pack(*be)
                units.append(dict(
                    hh=hh, pairs=pairs, k=k, q=q, v=v, be=be, gcol=gcol, g_last=g_last, grow_cum=grow_cum,
                    aqk=jnp.where(incl, qk * dm, 0.0),
                    a_levels=[block_diag(jnp.where(m, a_mat, 0.0).astype(BF16)) for m in level_masks[1:]],
                    inv=eye - jnp.where(level_masks[0], a_mat, 0.0)))
        yield

        for lvl in range(len(level_masks) - 1):
            inv_b = [un["inv"].astype(BF16) for un in units]
            t1 = [_dot(ib, un["a_levels"][lvl]) for ib, un in zip(inv_b, units)]
            t2 = [_dot(t.astype(BF16), block_diag(ib)) for t, ib in zip(t1, inv_b)]
            for un, t in zip(units, t2):
                un["inv"] = un["inv"] - t
            yield

        for un in units:
            k, q, v, be, gcol, g_last = un["k"], un["q"], un["v"], un["be"], un["gcol"], un["g_last"]
            ek = [jnp.exp(t) for t in gcol]
            rstack = jnp.concatenate(
                [jnp.concatenate([v[i] * be[i], k[i] * be[i] * ek[i]], axis=1) for i in range(DN_PACK)],
                axis=0).astype(BF16)
            uw = _dot(block_diag(un["inv"].astype(BF16)), rstack)
            e_row = jnp.exp(pack(*g_last, mask=left_half_row) - un["grow_cum"])
            out.append(dict(
                u=[uw[i * CHUNK:(i + 1) * CHUNK, :LANES] for i in range(DN_PACK)],
                wq=[jnp.concatenate([uw[i * CHUNK:(i + 1) * CHUNK, LANES:], q[i] * ek[i]], axis=0).astype(BF16)
                    for i in range(DN_PACK)],
                ka=[jnp.concatenate([kts[un["hh"], :, pair_cols(un["pairs"][d])] * e_row[:, d * LANES:(d + 1) * LANES],
                                     un["aqk"][:, d * LANES:(d + 1) * LANES]], axis=0).astype(BF16)
                    for d in range(2)],
                cd=[jnp.broadcast_to(jnp.exp(t), (8, LANES)) for t in g_last]))
        yield

    def scan_stages(stage_idx, res, states):
        for j in range(DN_GROUP):
            pairs = scan_pairs(stage_idx * DN_GROUP + j)
            for sub in range(2):
                for hh in heads:
                    unit = res[hh * DN_GROUP + j]
                    for d in range(2):
                        half = sub if d == 0 else 1 - sub
                        slot = 2 * d + half
                        rows = chunk_rows(2 * pairs[d] + half)
                        s = states[hh][d]
                        ws = _dot(unit["wq"][slot], s.astype(BF16))
                        v_new = unit["u"][slot] - ws[:CHUNK]
                        v_rows = jnp.concatenate([v_new, zeros_tile] if half == 0 else [zeros_tile, v_new],
                                                 axis=0).astype(BF16)
                        ka = _dot(unit["ka"][d], v_rows)
                        o_scr[hh, d, rows, :] = ws[CHUNK:] + ka[LANES:]
                        states[hh][d] = s * unit["cd"][slot][0:1, :] + ka[:LANES]
                yield

    def run(*gens):
        live = list(gens)
        while live:
            for gen in list(live):
                if next(gen, StopIteration) is StopIteration:
                    live.remove(gen)

    zero_state = jnp.zeros((DN_HEAD_DIM, DN_HEAD_DIM), F32)
    first = []
    run(local_stages(0, first))

    def pipeline(stage_idx, carry):
        res, states = carry
        states = [list(st) for st in states]
        new = []
        run(local_stages(stage_idx, new), scan_stages(stage_idx - 1, res, states))
        return new, states

    res, states = lax.fori_loop(1, n_stages, pipeline, (first, [[zero_state, zero_state] for _ in heads]))
    run(scan_stages(n_stages - 1, res, [list(st) for st in states]))

    for hh in heads:
        cols = slice(hh * LANES, (hh + 1) * LANES)
        for r0 in range(0, total, tr):
            o = o_scr[hh, 0, r0:r0 + tr, :] + o_scr[hh, 1, r0:r0 + tr, :]
            o = o * lax.rsqrt(jnp.mean(o * o, axis=1, keepdims=True) + RMS_EPS) * gain_ref[...]
            o_ref[0, r0:r0 + tr, cols] = (o * _silu(z_ref[0, r0:r0 + tr, cols].astype(F32))).astype(o_ref.dtype)


def _dn_call(p, ab, conv_w, a_log, dt_bias, gain, ctx_len):
    B, T, _ = p.shape
    hp = DN_HEADS_PER_STEP
    width = hp * LANES
    assert T % (2 * CHUNK * DN_GROUP) == 0 and ctx_len % (2 * CHUNK) == 0 and DN_HEADS % hp == 0

    def pad_row(v):
        return jnp.zeros((1, LANES), F32).at[0, :v.size].set(v.reshape(-1))

    def col(off):
        return lambda b, h: (b, 0, off // width + h)

    seq = lambda off: pl.BlockSpec((1, T, width), col(off))
    tap = lambda base: pl.BlockSpec((3, width), lambda b, h: (0, base * DN_HEADS // hp + h))
    vec = pl.BlockSpec((1, LANES), lambda b, h: (0, 0))
    return pl.pallas_call(
        functools.partial(_dn_kernel, ctx_len=ctx_len, total=T),
        grid=(B, DN_HEADS // hp),
        in_specs=[
            seq(OFF_CQ), seq(OFF_CK), seq(OFF_CV), seq(OFF_CZ),
            pl.BlockSpec((1, T, LANES), lambda b, h: (b, 0, 0)),
            tap(0), tap(1), tap(2),
            vec, vec, vec,
        ],
        out_specs=pl.BlockSpec((1, T, width), lambda b, h: (b, 0, h)),
        out_shape=jax.ShapeDtypeStruct((B, T, DN_WIDTH), BF16),
        scratch_shapes=[
            pltpu.VMEM((hp, T, LANES), F32), pltpu.VMEM((hp, T, LANES), F32), pltpu.VMEM((hp, T, LANES), F32),
            pltpu.VMEM((hp, LANES, T), F32),
            pltpu.VMEM((hp, 2, T, LANES), F32), pltpu.VMEM((hp, 2, T, LANES), F32),
            pltpu.VMEM((hp, 2, 8, T), F32),
            pltpu.VMEM((hp, 2, T, LANES), F32),
        ],
        compiler_params=_params("arbitrary", "arbitrary"),
        name="gated_deltanet",
    )(p, p, p, p, ab, conv_w, conv_w, conv_w, pad_row(a_log), pad_row(dt_bias), gain.reshape(1, LANES))


def _merge_kernel(ya_ref, yb_ref, yc_ref, g0_ref, g1_ref, g2_ref, x_ref, mb_ref, mc_ref, wb_ref, wo_ref,
                  lg_ref, lb_ref, o_ref, *, ctx_tiles, first_tile):
    i = pl.program_id(1) + first_tile

    def branch(y_ref, g_ref, r):
        proj = jnp.dot(y_ref[0], wb_ref[r], preferred_element_type=F32)
        return jax.nn.sigmoid(g_ref[0].astype(F32)) * proj

    merged = branch(ya_ref, g0_ref, 0) + branch(yb_ref, g1_ref, 1) + branch(yc_ref, g2_ref, 2)
    out = jnp.dot(merged.astype(BF16), wo_ref[...], preferred_element_type=F32)
    gate = jnp.where(i < ctx_tiles, mc_ref[0, 2:3, :], mb_ref[0, 2:3, :])
    t = DEEPNORM_ALPHA * x_ref[0] + gate * out
    mu = jnp.mean(t, axis=1, keepdims=True)
    cen = t - mu
    var = jnp.mean(cen * cen, axis=1, keepdims=True)
    o_ref[0] = cen * lax.rsqrt(var + LN_EPS) * lg_ref[...] + lb_ref[...]


def _merge_call(ya, yb, yc, p, xc, mods, w_branch, w_out, ln_g, ln_b, ctx_len, latent_only):
    B, T, _ = xc.shape
    ctx_row = B
    gcol = OFF_GATE // D_MODEL
    ctx_tiles = ctx_len // ROW_TILE
    first = ctx_tiles if latent_only else 0
    n_tiles = T // ROW_TILE - first
    y_spec = pl.BlockSpec((1, ROW_TILE, ATT_WIDTH), lambda b, i: (b, i + first, 0))
    gate_spec = lambda r: pl.BlockSpec((1, ROW_TILE, D_MODEL), lambda b, i: (b, i + first, gcol + r))
    vec = pl.BlockSpec((1, D_MODEL), lambda b, i: (0, 0))
    return pl.pallas_call(
        functools.partial(_merge_kernel, ctx_tiles=ctx_tiles, first_tile=first),
        grid=(B, n_tiles),
        in_specs=[
            y_spec, y_spec, y_spec,
            gate_spec(0), gate_spec(1), gate_spec(2),
            pl.BlockSpec((1, ROW_TILE, D_MODEL), lambda b, i: (b, i + first, 0)),
            pl.BlockSpec((1, 3, D_MODEL), lambda b, i: (b, 0, 0)),
            pl.BlockSpec((1, 3, D_MODEL), lambda b, i: (ctx_row, 0, 0)),
            pl.BlockSpec((N_BRANCH, ATT_WIDTH, D_MODEL), lambda b, i: (0, 0, 0)),
            pl.BlockSpec((D_MODEL, D_MODEL), lambda b, i: (0, 0)),
            vec, vec,
        ],
        out_specs=pl.BlockSpec((1, ROW_TILE, D_MODEL), lambda b, i: (b, i, 0)),
        out_shape=jax.ShapeDtypeStruct((B, n_tiles * ROW_TILE, D_MODEL), F32),
        compiler_params=_params("arbitrary", "arbitrary"),
        name="merge",
    )(ya, yb, yc, p, p, p, xc, mods, mods, w_branch, w_out, ln_g.reshape(1, D_MODEL), ln_b.reshape(1, D_MODEL))


def _permute_in_proj(w_in):
    pieces = [
        w_in[..., 0:512], w_in[..., 768:1280],
        w_in[..., 1280:2304], w_in[..., 2304:2816],
        w_in[..., 4352:4864],
        w_in[..., 4880:7952],
        w_in[..., 2816:4352],
        w_in[..., 512:640], w_in[..., 640:768],
        w_in[..., 4864:4880],
    ]
    pad = IN_WIDTH_PAD - sum(piece.shape[-1] for piece in pieces)
    pieces.append(jnp.zeros(w_in.shape[:-1] + (pad,), w_in.dtype))
    return jnp.concatenate(pieces, axis=-1).astype(BF16)


def kernel(x, c, ctx, c_ctx, w_ada, b_ada, w_in, a_sink, b_conv_w, b_conv_b, b_norm_g, b_norm_b,
           c_conv_w, c_a_log, c_dt_bias, c_norm_g, w_branch, w_out, ln_g, ln_b):
    B, n, _ = x.shape
    ctx_len = ctx.shape[1]
    assert ctx_len % ROW_TILE == 0 and n % ROW_TILE == 0 and n % GRID_W == 0

    xc = jnp.concatenate([ctx, x], axis=1)
    mod_rows = -(-(B + 1) // 8) * 8
    cc = jnp.concatenate([c, c_ctx[None, :], jnp.zeros((mod_rows - B - 1, D_MODEL), F32)], axis=0)
    mods_all = _ada_call(cc, w_ada, b_ada).reshape(DEPTH, mod_rows, 3, D_MODEL)
    w_perm = _permute_in_proj(w_in)
    wb16 = w_branch.astype(BF16)
    wo16 = w_out.astype(BF16)
    cos, sin = _rope_tables(n, ctx_len)

    for l in range(DEPTH):
        mods = mods_all[l]
        p, ab = _inproj_call(xc, mods, w_perm[l], ctx_len, BF16)
        ya = _attn_call(p, a_sink[l], cos, sin, ctx_len)
        yb = _conv_call(p, b_conv_w[l], b_conv_b[l], b_norm_g[l], b_norm_b[l], ctx_len)
        yc = _dn_call(p, ab, c_conv_w[l], c_a_log[l], c_dt_bias[l], c_norm_g[l], ctx_len)
        xc = _merge_call(ya, yb, yc, p, xc, mods, wb16[l], wo16[l], ln_g[l], ln_b[l], ctx_len,
                         latent_only=l == DEPTH - 1)
    return xc
```

```python
import functools

import jax
import jax.numpy as jnp
from jax import lax
from jax.experimental import pallas as pl
from jax.experimental.pallas import tpu as pltpu

F32 = jnp.float32
BF16 = jnp.bfloat16
HIGHEST = lax.Precision.HIGHEST

LANES = 128
VMEM_LIMIT = 56 * 1024 * 1024

D_MODEL = 1024
DEPTH = 4
GRID_W = 64
HEAD_DIM = 64
ATT_Q_HEADS = 8
ATT_KV_HEADS = 2
ATT_GROUP = ATT_Q_HEADS // ATT_KV_HEADS
ATT_WIDTH = 512
BLOCK = 128
ROPE_BASE = 10000.0
ROPE_PAIRS = HEAD_DIM // 4
CONV_WIDTH = 512
CONV_TAPS = 31
CONV_HALO = 16
DN_HEADS = 4
DN_HEAD_DIM = 128
DN_WIDTH = 512
CHUNK = 64
N_BRANCH = 3
DEEPNORM_ALPHA = (2 * DEPTH) ** 0.25
LN_EPS = 1e-5
RMS_EPS = 1e-6
NEG_INF = -1e30

OFF_AQ, OFF_AZ = 0, 512
OFF_BA, OFF_BB, OFF_BZ = 1024, 1536, 2048
OFF_CZ = 2560
OFF_GATE = 3072
OFF_CQ, OFF_CK, OFF_CV = 6144, 6656, 7168
OFF_AK, OFF_AV = 7680, 7808
OFF_AB = 7936
IN_WIDTH_PAD = 8192
IN_TILE_N = 1024
ROW_TILE = 256


def _silu(t):
    return t * jax.nn.sigmoid(t)


def _params(*sem):
    return pltpu.CompilerParams(dimension_semantics=sem, vmem_limit_bytes=VMEM_LIMIT)


def _ada_kernel(cc_ref, w_ref, b_ref, o_ref):
    s = _silu(cc_ref[...])
    o_ref[0] = jnp.dot(s, w_ref[0], precision=HIGHEST, preferred_element_type=F32) + b_ref[0]


def _ada_call(cc, w_ada, b_ada):
    rows = cc.shape[0]
    return pl.pallas_call(
        _ada_kernel,
        grid=(DEPTH, 3),
        in_specs=[
            pl.BlockSpec((rows, D_MODEL), lambda l, j: (0, 0)),
            pl.BlockSpec((1, D_MODEL, D_MODEL), lambda l, j: (l, 0, j)),
            pl.BlockSpec((1, 1, D_MODEL), lambda l, j: (l, 0, j)),
        ],
        out_specs=pl.BlockSpec((1, rows, D_MODEL), lambda l, j: (l, 0, j)),
        out_shape=jax.ShapeDtypeStruct((DEPTH, rows, 3 * D_MODEL), F32),
        compiler_params=_params("arbitrary", "arbitrary"),
        name="ada_mod",
    )(cc, w_ada, b_ada.reshape(DEPTH, 1, 3 * D_MODEL))


def _inproj_kernel(x_ref, mb_ref, mc_ref, w_ref, p_ref, ab_ref, u_scr, *, tm, ctx_len):
    i = pl.program_id(1)
    j = pl.program_id(2)

    @pl.when(j == 0)
    def _():
        row = lax.broadcasted_iota(jnp.int32, (tm, 1), 0) + i * tm
        is_ctx = row < ctx_len
        shift = jnp.where(is_ctx, mc_ref[0, 0:1, :], mb_ref[0, 0:1, :])
        scale = jnp.where(is_ctx, mc_ref[0, 1:2, :], mb_ref[0, 1:2, :])
        u_scr[...] = (x_ref[0] * (1.0 + scale) + shift).astype(BF16)

    w = w_ref[:, pl.ds(pl.multiple_of(j * IN_TILE_N, IN_TILE_N), IN_TILE_N)]
    p = jnp.dot(u_scr[...], w, preferred_element_type=F32)
    p_ref[0] = p.astype(p_ref.dtype)

    @pl.when(j == OFF_AB // IN_TILE_N)
    def _():
        ab_ref[0] = p[:, OFF_AB % IN_TILE_N:OFF_AB % IN_TILE_N + LANES]


def _inproj_call(xc, mods, w, ctx_len, p_dtype):
    B, T, _ = xc.shape
    tm = T // 2
    nj = IN_WIDTH_PAD // IN_TILE_N
    ctx_row = B
    return pl.pallas_call(
        functools.partial(_inproj_kernel, tm=tm, ctx_len=ctx_len),
        grid=(B, T // tm, nj),
        in_specs=[
            pl.BlockSpec((1, tm, D_MODEL), lambda b, i, j: (b, i, 0)),
            pl.BlockSpec((1, 3, D_MODEL), lambda b, i, j: (b, 0, 0)),
            pl.BlockSpec((1, 3, D_MODEL), lambda b, i, j: (ctx_row, 0, 0)),
            pl.BlockSpec((D_MODEL, IN_WIDTH_PAD), lambda b, i, j: (0, 0)),
        ],
        out_specs=[
            pl.BlockSpec((1, tm, IN_TILE_N), lambda b, i, j: (b, i, j)),
            pl.BlockSpec((1, tm, LANES), lambda b, i, j: (b, i, 0)),
        ],
        out_shape=[
            jax.ShapeDtypeStruct((B, T, IN_WIDTH_PAD), p_dtype),
            jax.ShapeDtypeStruct((B, T, LANES), F32),
        ],
        scratch_shapes=[pltpu.VMEM((tm, D_MODEL), BF16)],
        compiler_params=_params("arbitrary", "arbitrary", "arbitrary"),
        name="in_proj",
    )(xc, mods, mods, w)


def _rope(t, cos, sin_signed):
    lane = lax.broadcasted_iota(jnp.int32, (1, LANES), 1)
    partner = jnp.where((lane % 32) < 16, pltpu.roll(t, LANES - 16, 1), pltpu.roll(t, 16, 1))
    return t * cos + partner * sin_signed


def _attn_kernel(sink_ref, q_ref, z_ref, k_ref, v_ref, cq_ref, sq_ref, ck_ref, sk_ref, o_ref, kr_scr,
                 *, ctx_len, n_blocks):
    i = pl.program_id(1)
    ctx_blocks = ctx_len // BLOCK

    @pl.when(i == 0)
    def _():
        kr_scr[...] = _rope(k_ref[0].astype(F32), ck_ref[...], sk_ref[...]).astype(BF16)

    q = q_ref[0].astype(F32) * (HEAD_DIM ** -0.5)
    cq = cq_ref[...]
    sq = sq_ref[...]
    q_pairs = [_rope(q[:, LANES * t:LANES * (t + 1)], cq, sq) for t in range(ATT_WIDTH // LANES)]

    def q_head(hq):
        off = (hq % 2) * HEAD_DIM
        return q_pairs[hq // 2][:, off:off + HEAD_DIM]

    left = pl.multiple_of(jnp.maximum(i - 1, 0) * BLOCK, BLOCK)
    mid = pl.multiple_of(i * BLOCK, BLOCK)
    right = pl.multiple_of(jnp.minimum(i + 1, n_blocks - 1) * BLOCK, BLOCK)
    kcat = jnp.concatenate([kr_scr[0:ctx_len, :], kr_scr[pl.ds(left, BLOCK), :],
                            kr_scr[pl.ds(mid, BLOCK), :], kr_scr[pl.ds(right, BLOCK), :]], axis=0)
    vcat = jnp.concatenate([v_ref[0, 0:ctx_len, :], v_ref[0, pl.ds(left, BLOCK), :],
                            v_ref[0, pl.ds(mid, BLOCK), :], v_ref[0, pl.ds(right, BLOCK), :]],
                           axis=0).astype(BF16)

    rows = ATT_GROUP * BLOCK
    is_lat = i >= ctx_blocks
    has_left = i >= ctx_blocks + 1
    has_right = jnp.logical_and(is_lat, i <= n_blocks - 2)
    rq = lax.broadcasted_iota(jnp.int32, (rows, BLOCK), 0) % BLOCK
    kc = lax.broadcasted_iota(jnp.int32, (rows, BLOCK), 1)
    grp = lax.broadcasted_iota(jnp.int32, (rows, 1), 0) // BLOCK
    neg_left = jnp.where(has_left, 0.0, NEG_INF)
    neg_mid = jnp.where(is_lat, 0.0, NEG_INF)
    neg_right = jnp.where(has_right, 0.0, NEG_INF)
    bias_left = jnp.where(kc >= rq, neg_left, NEG_INF)
    bias_right = jnp.where(kc <= rq, neg_right, NEG_INF)

    def masked(s):
        band = s[:, ctx_len:]
        return jnp.concatenate([s[:, :ctx_len], band[:, :BLOCK] + bias_left, band[:, BLOCK:2 * BLOCK] + neg_mid,
                                band[:, 2 * BLOCK:] + bias_right], axis=1)

    heads = range(ATT_KV_HEADS)
    qh = [jnp.concatenate([q_head(h * ATT_GROUP + g) for g in range(ATT_GROUP)], axis=0).astype(BF16) for h in heads]
    s = [masked(lax.dot_general(qh[h], kcat[:, h * HEAD_DIM:(h + 1) * HEAD_DIM], (((1,), (1,)), ((), ())),
                                preferred_element_type=F32)) for h in heads]
    sk = []
    for h in heads:
        col = jnp.zeros((rows, 1), F32)
        for g in range(ATT_GROUP):
            col = jnp.where(grp == g, sink_ref[h * ATT_GROUP + g], col)
        sk.append(col)
    m = [jnp.maximum(jnp.max(s[h], axis=1, keepdims=True), sk[h]) for h in heads]
    p = [jnp.exp(s[h] - m[h]) for h in heads]
    den = [jnp.sum(p[h], axis=1, keepdims=True) + jnp.exp(sk[h] - m[h]) for h in heads]
    o = [jnp.dot(p[h].astype(BF16), vcat[:, h * HEAD_DIM:(h + 1) * HEAD_DIM], preferred_element_type=F32) / den[h]
         for h in heads]
    att = jnp.concatenate([o[h][g * BLOCK:(g + 1) * BLOCK] for h in heads for g in range(ATT_GROUP)], axis=1)
    o_ref[0] = (att * _silu(z_ref[0].astype(F32))).astype(o_ref.dtype)


def _attn_call(p, sink, cos, sin, ctx_len):
    B, T, _ = p.shape
    nb = T // BLOCK
    return pl.pallas_call(
        functools.partial(_attn_kernel, ctx_len=ctx_len, n_blocks=nb),
        grid=(B, nb),
        in_specs=[
            pl.BlockSpec(memory_space=pltpu.SMEM),
            pl.BlockSpec((1, BLOCK, ATT_WIDTH), lambda b, i: (b, i, OFF_AQ // ATT_WIDTH)),
            pl.BlockSpec((1, BLOCK, ATT_WIDTH), lambda b, i: (b, i, OFF_AZ // ATT_WIDTH)),
            pl.BlockSpec((1, T, LANES), lambda b, i: (b, 0, OFF_AK // LANES)),
            pl.BlockSpec((1, T, LANES), lambda b, i: (b, 0, OFF_AV // LANES)),
            pl.BlockSpec((BLOCK, LANES), lambda b, i: (i, 0)),
            pl.BlockSpec((BLOCK, LANES), lambda b, i: (i, 0)),
            pl.BlockSpec((T, LANES), lambda b, i: (0, 0)),
            pl.BlockSpec((T, LANES), lambda b, i: (0, 0)),
        ],
        out_specs=pl.BlockSpec((1, BLOCK, ATT_WIDTH), lambda b, i: (b, i, 0)),
        out_shape=jax.ShapeDtypeStruct((B, T, ATT_WIDTH), BF16),
        scratch_shapes=[pltpu.VMEM((T, LANES), BF16)],
        compiler_params=_params("arbitrary", "arbitrary"),
        name="attention",
    )(sink, p, p, p, p, cos, sin, cos, sin)


def _rope_tables(n, ctx_len):
    pos = jnp.arange(n)
    row = (pos // GRID_W).astype(F32)
    colp = (pos % GRID_W).astype(F32)
    inv = ROPE_BASE ** (-jnp.arange(ROPE_PAIRS, dtype=F32) / ROPE_PAIRS)
    lane = jnp.arange(LANES)
    axis = (lane % HEAD_DIM) // (2 * ROPE_PAIRS)
    pair = lane % ROPE_PAIRS
    ang = jnp.where(axis[None, :] == 0, row[:, None], colp[:, None]) * inv[pair][None, :]
    sign = jnp.where((lane % (2 * ROPE_PAIRS)) < ROPE_PAIRS, -1.0, 1.0).astype(F32)
    cos = jnp.concatenate([jnp.ones((ctx_len, LANES), F32), jnp.cos(ang)], axis=0)
    sin = jnp.concatenate([jnp.zeros((ctx_len, LANES), F32), jnp.sin(ang) * sign[None, :]], axis=0)
    return cos, sin


def _conv_kernel(a_ref, b_ref, z_ref, ap_ref, bp_ref, an_ref, bn_ref, w_ref, cb_ref, g_ref, nb_ref,
                 o_ref, h_scr, shift_scr, *, ctx_tiles, n_tiles):
    i = pl.program_id(1)

    def glu(a, b):
        return a.astype(F32) * jax.nn.sigmoid(b.astype(F32))

    has_prev = jnp.logical_and(i != 0, i != ctx_tiles)
    has_next = jnp.logical_and(i != ctx_tiles - 1, i != n_tiles - 1)
    h_scr[0:CONV_HALO, :] = jnp.where(has_prev, glu(ap_ref[0], bp_ref[0]), 0.0)
    h_scr[CONV_HALO:CONV_HALO + ROW_TILE, :] = glu(a_ref[0], b_ref[0])
    h_scr[CONV_HALO + ROW_TILE:, :] = jnp.where(has_next, glu(an_ref[0], bn_ref[0]), 0.0)

    base = CONV_HALO - CONV_TAPS // 2
    span = ROW_TILE + CONV_HALO + 8
    acc = jnp.zeros((ROW_TILE, CONV_WIDTH), F32)
    for r in range(8):
        if r:
            shift_scr[...] = h_scr[r:r + span, :]
        src = shift_scr if r else h_scr
        for off in range(r, base + CONV_TAPS, 8):
            k = off - base
            if k >= 0:
                acc = acc + w_ref[k:k + 1, :] * src[off - r:off - r + ROW_TILE, :]
    acc = acc + cb_ref[...]
    mu = jnp.mean(acc, axis=1, keepdims=True)
    cen = acc - mu
    var = jnp.mean(cen * cen, axis=1, keepdims=True)
    hn = cen * lax.rsqrt(var + LN_EPS) * g_ref[...] + nb_ref[...]
    o_ref[0] = (_silu(hn) * _silu(z_ref[0].astype(F32))).astype(o_ref.dtype)


def _conv_call(p, conv_w, conv_b, norm_g, norm_b, ctx_len):
    B, T, _ = p.shape
    nt = T // ROW_TILE
    hpt = ROW_TILE // CONV_HALO
    last_halo = T // CONV_HALO - 1
    ca, cb_, cz = OFF_BA // CONV_WIDTH, OFF_BB // CONV_WIDTH, OFF_BZ // CONV_WIDTH

    def prev_map(col):
        return lambda b, i: (b, jnp.maximum(i * hpt - 1, 0), col)

    def next_map(col):
        return lambda b, i: (b, jnp.minimum((i + 1) * hpt, last_halo), col)

    row = lambda v: v.reshape(1, CONV_WIDTH)
    vec_spec = pl.BlockSpec((1, CONV_WIDTH), lambda b, i: (0, 0))
    return pl.pallas_call(
        functools.partial(_conv_kernel, ctx_tiles=ctx_len // ROW_TILE, n_tiles=nt),
        grid=(B, nt),
        in_specs=[
            pl.BlockSpec((1, ROW_TILE, CONV_WIDTH), lambda b, i: (b, i, ca)),
            pl.BlockSpec((1, ROW_TILE, CONV_WIDTH), lambda b, i: (b, i, cb_)),
            pl.BlockSpec((1, ROW_TILE, CONV_WIDTH), lambda b, i: (b, i, cz)),
            pl.BlockSpec((1, CONV_HALO, CONV_WIDTH), prev_map(ca)),
            pl.BlockSpec((1, CONV_HALO, CONV_WIDTH), prev_map(cb_)),
            pl.BlockSpec((1, CONV_HALO, CONV_WIDTH), next_map(ca)),
            pl.BlockSpec((1, CONV_HALO, CONV_WIDTH), next_map(cb_)),
            pl.BlockSpec((CONV_TAPS, CONV_WIDTH), lambda b, i: (0, 0)),
            vec_spec, vec_spec, vec_spec,
        ],
        out_specs=pl.BlockSpec((1, ROW_TILE, CONV_WIDTH), lambda b, i: (b, i, 0)),
        out_shape=jax.ShapeDtypeStruct((B, T, CONV_WIDTH), BF16),
        scratch_shapes=[pltpu.VMEM((ROW_TILE + 2 * CONV_HALO, CONV_WIDTH), F32),
                        pltpu.VMEM((ROW_TILE + CONV_HALO + 8, CONV_WIDTH), F32)],
        compiler_params=_params("arbitrary", "arbitrary"),
        name="conformer_conv",
    )(p, p, p, p, p, p, p, conv_w, row(conv_b), row(norm_g), row(norm_b))


DN_PACK = 4
DN_GROUP = 3
DN_HEADS_PER_STEP = 2


def _split_bf16(t):
    hi = t.astype(BF16)
    return hi, (t - hi.astype(F32)).astype(BF16)


def _dot(a, b):
    return jnp.dot(a, b, preferred_element_type=F32)


def _dn_kernel(q_ref, k_ref, v_ref, z_ref, ab_ref, wq_ref, wk_ref, wv_ref, alog_ref, dtb_ref, gain_ref,
               o_ref, qs, ks, vs, kts, gb, bb, grow, o_scr, *, ctx_len, total):
    head0 = pl.program_id(1) * DN_HEADS_PER_STEP
    n_pairs = total // (2 * CHUNK)
    ctx_pairs = ctx_len // (2 * CHUNK)
    n_stages = n_pairs // DN_GROUP
    tr = ROW_TILE
    wide = DN_PACK * CHUNK
    heads = range(DN_HEADS_PER_STEP)
    lane = lax.broadcasted_iota(jnp.int32, (1, LANES), 1)
    rowi = lax.broadcasted_iota(jnp.int32, (tr, 1), 0)

    def short_conv(ref, w_ref, r0, hh):
        cols = slice(hh * LANES, (hh + 1) * LANES)
        x = ref[0, r0:r0 + tr, cols].astype(F32)
        zero = jnp.zeros((1, LANES), F32)
        prev_row = zero if r0 in (0, ctx_len) else ref[0, r0 - 16:r0, cols].astype(F32)[15:16]
        next_row = zero if r0 + tr in (ctx_len, total) else ref[0, r0 + tr:r0 + tr + 16, cols].astype(F32)[0:1]
        xp = jnp.where(rowi == 0, prev_row, pltpu.roll(x, 1, 0))
        xn = jnp.where(rowi == tr - 1, next_row, pltpu.roll(x, tr - 1, 0))
        y = w_ref[0:1, cols] * xp + w_ref[1:2, cols] * x + w_ref[2:3, cols] * xn
        return _silu(y)

    def l2norm(t):
        return t * lax.rsqrt(jnp.sum(t * t, axis=1, keepdims=True) + RMS_EPS)

    def lane_pick(t, idx):
        return jnp.broadcast_to(jnp.sum(jnp.where(lane == idx, t, 0.0), axis=1, keepdims=True), t.shape)

    for r0 in range(0, total, tr):
        a = ab_ref[0, r0:r0 + tr, :]
        pre = a + dtb_ref[...]
        softplus = jnp.maximum(pre, 0.0) + jnp.log1p(jnp.exp(-jnp.abs(pre)))
        g_all = -jnp.exp(alog_ref[...]) * softplus
        b_all = jax.nn.sigmoid(a)
        for hh in heads:
            qs[hh, r0:r0 + tr, :] = l2norm(short_conv(q_ref, wq_ref, r0, hh)) * (DN_HEAD_DIM ** -0.5)
            kn = l2norm(short_conv(k_ref, wk_ref, r0, hh))
            ks[hh, r0:r0 + tr, :] = kn
            kts[hh, :, r0:r0 + tr] = kn.T
            vs[hh, r0:r0 + tr, :] = short_conv(v_ref, wv_ref, r0, hh)
            for d in range(2):
                g_sel = lane_pick(g_all, d * DN_HEADS + head0 + hh)
                gb[hh, d, r0:r0 + tr, :] = g_sel
                grow[hh, d, :, r0:r0 + tr] = g_sel.T[0:8, :]
                bb[hh, d, r0:r0 + tr, :] = lane_pick(b_all, 2 * DN_HEADS + d * DN_HEADS + head0 + hh)

    prow = lax.broadcasted_iota(jnp.int32, (CHUNK, wide), 0)
    plane = lax.broadcasted_iota(jnp.int32, (CHUNK, wide), 1)
    pcol = plane % CHUNK
    ahead = jnp.where(plane < 2 * CHUNK, prow - pcol, pcol - prow)
    incl = ahead >= 0
    strict = ahead > 0
    eye = jnp.where(prow == pcol, 1.0, 0.0).astype(F32)
    level_masks = tuple(jnp.logical_and(prow // (2 * s) == pcol // (2 * s), prow // s != pcol // s)
                        for s in (1, 2, 4, 8, 16, 32))
    slot_masks = tuple(jnp.where(plane // CHUNK == i, 1.0, 0.0).astype(BF16) for i in range(DN_PACK))
    ci = lax.broadcasted_iota(jnp.int32, (CHUNK, CHUNK), 0)
    cj = lax.broadcasted_iota(jnp.int32, (CHUNK, CHUNK), 1)
    lower_ones = jnp.where(cj <= ci, 1.0, 0.0).astype(BF16)
    wr = lax.broadcasted_iota(jnp.int32, (wide, wide), 0)
    wc = lax.broadcasted_iota(jnp.int32, (wide, wide), 1)
    same_slot = wr // CHUNK == wc // CHUNK
    along = jnp.where(wr < 2 * CHUNK, wc - wr, wr - wc) >= 0
    row_cumsum = jnp.where(jnp.logical_and(same_slot, along), 1.0, 0.0).astype(BF16)
    left_half = lax.broadcasted_iota(jnp.int32, (CHUNK, LANES), 1) < CHUNK
    left_half_row = lane < CHUNK
    zeros_tile = jnp.zeros((CHUNK, LANES), F32)

    def block_diag(y):
        return jnp.concatenate([y * m for m in slot_masks], axis=0)

    def pack(t0, t1, t2, t3, mask=left_half):
        return jnp.concatenate([jnp.where(mask, t0, t1), jnp.where(mask, t2, t3)], axis=1)

    def scan_pairs(step):
        return step, jnp.where(step < ctx_pairs, ctx_pairs - 1 - step, n_pairs - 1 + ctx_pairs - step)

    def chunk_rows(chunk):
        return pl.ds(pl.multiple_of(chunk * CHUNK, CHUNK), CHUNK)

    def pair_cols(pair):
        return pl.ds(pl.multiple_of(pair * LANES, LANES), LANES)

    def local_stages(stage_idx, out):
        units = []
        for hh in heads:
            for j in range(DN_GROUP):
                pairs = scan_pairs(stage_idx * DN_GROUP + j)
                rows = [chunk_rows(2 * pairs[d] + i) for d in range(2) for i in range(2)]
                k = [ks[hh, r, :] for r in rows]
                q = [qs[hh, r, :] for r in rows]
                v = [vs[hh, r, :] for r in rows]
                g = [gb[hh, i // 2, rows[i], :] for i in range(DN_PACK)]
                be = [bb[hh, i // 2, rows[i], :] for i in range(DN_PACK)]
                g_hi, g_lo = _split_bf16(jnp.concatenate(g, axis=1))
                prefix = _dot(lower_ones, g_hi) + _dot(lower_ones, g_lo)
                gcol, g_last = [], []
                for i in range(DN_PACK):
                    pf = prefix[:, i * LANES:(i + 1) * LANES]
                    tot = pf[CHUNK - 1:CHUNK, :]
                    gcol.append(pf if i < 2 else tot - pf + g[i])
                    g_last.append(tot)
                r_hi, r_lo = _split_bf16(jnp.concatenate(
                    [jnp.broadcast_to(grow[hh, d, 0:1, pair_cols(pairs[d])], (8, LANES)) for d in range(2)], axis=1))
                grow_cum = (_dot(r_hi, row_cumsum) + _dot(r_lo, row_cumsum))[0:1, :]
                dm = jnp.exp(jnp.where(incl, pack(*gcol) - grow_cum, NEG_INF))
                kkqk = []
                for d in range(2):
                    k0, k1, q0, q1 = k[2 * d], k[2 * d + 1], q[2 * d], q[2 * d + 1]
                    lhs = jnp.concatenate([jnp.concatenate([k0, q0], axis=0),
                                           jnp.concatenate([k1, q1], axis=0)], axis=1).astype(BF16)
                    rhs = jnp.concatenate([jnp.concatenate([k0, zeros_tile], axis=1),
                                           jnp.concatenate([zeros_tile, k1], axis=1)], axis=0).astype(BF16)
                    kkqk.append(lax.dot_general(lhs, rhs, (((1,), (1,)), ((), ())), preferred_element_type=F32))
                kk = jnp.concatenate([t[:CHUNK] for t in kkqk], axis=1)
                qk = jnp.concatenate([t[CHUNK:] for t in kkqk], axis=1)
                a_mat = jnp.where(strict, kk * dm, 0.0) * pack(*be)
                units.append(dict(
                    hh=hh, pairs=pairs, rows=rows, gcol=gcol, g_last=g_last, grow_cum=grow_cum,
                    aqk=jnp.where(incl, qk * dm, 0.0),
                    a_bd=block_diag(a_mat.astype(BF16)),
                    inv=eye - jnp.where(level_masks[0], a_mat, 0.0)))
        yield

        for m in level_masks[1:]:
            inv_b = [un["inv"].astype(BF16) for un in units]
            t1 = [_dot(ib, un["a_bd"]) for ib, un in zip(inv_b, units)]
            t2 = [_dot(t.astype(BF16), block_diag(ib)) for t, ib in zip(t1, inv_b)]
            for un, t in zip(units, t2):
                un["inv"] = un["inv"] - jnp.where(m, t, 0.0)
            yield

        for un in units:
            hh, rows, gcol, g_last = un["hh"], un["rows"], un["gcol"], un["g_last"]
            k = [ks[hh, r, :] for r in rows]
            q = [qs[hh, r, :] for r in rows]
            v = [vs[hh, r, :] for r in rows]
            be = [bb[hh, i // 2, rows[i], :] for i in range(DN_PACK)]
            ek = [jnp.exp(t) for t in gcol]
            rstack = jnp.concatenate(
                [jnp.concatenate([v[i] * be[i], k[i] * be[i] * ek[i]], axis=1) for i in range(DN_PACK)],
                axis=0).astype(BF16)
            uw = _dot(block_diag(un["inv"].astype(BF16)), rstack)
            e_row = jnp.exp(pack(*g_last, mask=left_half_row) - un["grow_cum"])
            out.append(dict(
                u=[uw[i * CHUNK:(i + 1) * CHUNK, :LANES] for i in range(DN_PACK)],
                wq=[jnp.concatenate([uw[i * CHUNK:(i + 1) * CHUNK, LANES:], q[i] * ek[i]], axis=0).astype(BF16)
                    for i in range(DN_PACK)],
                ka=[jnp.concatenate([kts[un["hh"], :, pair_cols(un["pairs"][d])] * e_row[:, d * LANES:(d + 1) * LANES],
                                     un["aqk"][:, d * LANES:(d + 1) * LANES]], axis=0).astype(BF16)
                    for d in range(2)],
                cd=[jnp.broadcast_to(jnp.exp(t), (8, LANES)) for t in g_last]))
        yield

    def scan_stages(stage_idx, res, states):
        for j in range(DN_GROUP):
            pairs = scan_pairs(stage_idx * DN_GROUP + j)
            for sub in range(2):
                for hh in heads:
                    unit = res[hh * DN_GROUP + j]
                    for d in range(2):
                        half = sub if d == 0 else 1 - sub
                        slot = 2 * d + half
                        rows = chunk_rows(2 * pairs[d] + half)
                        s = states[hh][d]
                        ws = _dot(unit["wq"][slot], s.astype(BF16))
                        v_new = unit["u"][slot] - ws[:CHUNK]
                        v_rows = jnp.concatenate([v_new, zeros_tile] if half == 0 else [zeros_tile, v_new],
                                                 axis=0).astype(BF16)
                        ka = _dot(unit["ka"][d], v_rows)
                        o_scr[hh, d, rows, :] = ws[CHUNK:] + ka[LANES:]
                        states[hh][d] = s * unit["cd"][slot][0:1, :] + ka[:LANES]
                yield

    def run(*gens):
        live = list(gens)
        while live:
            for gen in list(live):
                if next(gen, StopIteration) is StopIteration:
                    live.remove(gen)

    zero_state = jnp.zeros((DN_HEAD_DIM, DN_HEAD_DIM), F32)
    first = []
    run(local_stages(0, first))

    def pipeline(stage_idx, carry):
        res, states = carry
        states = [list(st) for st in states]
        new = []
        run(local_stages(stage_idx, new), scan_stages(stage_idx - 1, res, states))
        return new, states

    res, states = lax.fori_loop(1, n_stages, pipeline, (first, [[zero_state, zero_state] for _ in heads]))
    run(scan_stages(n_stages - 1, res, [list(st) for st in states]))

    for hh in heads:
        cols = slice(hh * LANES, (hh + 1) * LANES)
        for r0 in range(0, total, tr):
            o = o_scr[hh, 0, r0:r0 + tr, :] + o_scr[hh, 1, r0:r0 + tr, :]
            o = o * lax.rsqrt(jnp.mean(o * o, axis=1, keepdims=True) + RMS_EPS) * gain_ref[...]
            o_ref[0, r0:r0 + tr, cols] = (o * _silu(z_ref[0, r0:r0 + tr, cols].astype(F32))).astype(o_ref.dtype)


def _dn_call(p, ab, conv_w, a_log, dt_bias, gain, ctx_len):
    B, T, _ = p.shape
    hp = DN_HEADS_PER_STEP
    width = hp * LANES
    assert T % (2 * CHUNK * DN_GROUP) == 0 and ctx_len % (2 * CHUNK) == 0 and DN_HEADS % hp == 0

    def pad_row(v):
        return jnp.zeros((1, LANES), F32).at[0, :v.size].set(v.reshape(-1))

    def col(off):
        return lambda b, h: (b, 0, off // width + h)

    seq = lambda off: pl.BlockSpec((1, T, width), col(off))
    tap = lambda base: pl.BlockSpec((3, width), lambda b, h: (0, base * DN_HEADS // hp + h))
    vec = pl.BlockSpec((1, LANES), lambda b, h: (0, 0))
    return pl.pallas_call(
        functools.partial(_dn_kernel, ctx_len=ctx_len, total=T),
        grid=(B, DN_HEADS // hp),
        in_specs=[
            seq(OFF_CQ), seq(OFF_CK), seq(OFF_CV), seq(OFF_CZ),
            pl.BlockSpec((1, T, LANES), lambda b, h: (b, 0, 0)),
            tap(0), tap(1), tap(2),
            vec, vec, vec,
        ],
        out_specs=pl.BlockSpec((1, T, width), lambda b, h: (b, 0, h)),
        out_shape=jax.ShapeDtypeStruct((B, T, DN_WIDTH), BF16),
        scratch_shapes=[
            pltpu.VMEM((hp, T, LANES), F32), pltpu.VMEM((hp, T, LANES), F32), pltpu.VMEM((hp, T, LANES), F32),
            pltpu.VMEM((hp, LANES, T), F32),
            pltpu.VMEM((hp, 2, T, LANES), F32), pltpu.VMEM((hp, 2, T, LANES), F32),
            pltpu.VMEM((hp, 2, 8, T), F32),
            pltpu.VMEM((hp, 2, T, LANES), F32),
        ],
        compiler_params=_params("arbitrary", "arbitrary"),
        name="gated_deltanet",
    )(p, p, p, p, ab, conv_w, conv_w, conv_w, pad_row(a_log), pad_row(dt_bias), gain.reshape(1, LANES))


def _merge_kernel(ya_ref, yb_ref, yc_ref, g0_ref, g1_ref, g2_ref, x_ref, mb_ref, mc_ref, wb_ref, wo_ref,
                  lg_ref, lb_ref, o_ref, *, ctx_tiles, first_tile):
    i = pl.program_id(1) + first_tile

    def branch(y_ref, g_ref, r):
        proj = jnp.dot(y_ref[0], wb_ref[r], preferred_element_type=F32)
        return jax.nn.sigmoid(g_ref[0].astype(F32)) * proj

    merged = branch(ya_ref, g0_ref, 0) + branch(yb_ref, g1_ref, 1) + branch(yc_ref, g2_ref, 2)
    out = jnp.dot(merged.astype(BF16), wo_ref[...], preferred_element_type=F32)
    gate = jnp.where(i < ctx_tiles, mc_ref[0, 2:3, :], mb_ref[0, 2:3, :])
    t = DEEPNORM_ALPHA * x_ref[0] + gate * out
    mu = jnp.mean(t, axis=1, keepdims=True)
    cen = t - mu
    var = jnp.mean(cen * cen, axis=1, keepdims=True)
    o_ref[0] = cen * lax.rsqrt(var + LN_EPS) * lg_ref[...] + lb_ref[...]


def _merge_call(ya, yb, yc, p, xc, mods, w_branch, w_out, ln_g, ln_b, ctx_len, latent_only):
    B, T, _ = xc.shape
    ctx_row = B
    gcol = OFF_GATE // D_MODEL
    ctx_tiles = ctx_len // ROW_TILE
    first = ctx_tiles if latent_only else 0
    n_tiles = T // ROW_TILE - first
    y_spec = pl.BlockSpec((1, ROW_TILE, ATT_WIDTH), lambda b, i: (b, i + first, 0))
    gate_spec = lambda r: pl.BlockSpec((1, ROW_TILE, D_MODEL), lambda b, i: (b, i + first, gcol + r))
    vec = pl.BlockSpec((1, D_MODEL), lambda b, i: (0, 0))
    return pl.pallas_call(
        functools.partial(_merge_kernel, ctx_tiles=ctx_tiles, first_tile=first),
        grid=(B, n_tiles),
        in_specs=[
            y_spec, y_spec, y_spec,
            gate_spec(0), gate_spec(1), gate_spec(2),
            pl.BlockSpec((1, ROW_TILE, D_MODEL), lambda b, i: (b, i + first, 0)),
            pl.BlockSpec((1, 3, D_MODEL), lambda b, i: (b, 0, 0)),
            pl.BlockSpec((1, 3, D_MODEL), lambda b, i: (ctx_row, 0, 0)),
            pl.BlockSpec((N_BRANCH, ATT_WIDTH, D_MODEL), lambda b, i: (0, 0, 0)),
            pl.BlockSpec((D_MODEL, D_MODEL), lambda b, i: (0, 0)),
            vec, vec,
        ],
        out_specs=pl.BlockSpec((1, ROW_TILE, D_MODEL), lambda b, i: (b, i, 0)),
        out_shape=jax.ShapeDtypeStruct((B, n_tiles * ROW_TILE, D_MODEL), F32),
        compiler_params=_params("arbitrary", "arbitrary"),
        name="merge",
    )(ya, yb, yc, p, p, p, xc, mods, mods, w_branch, w_out, ln_g.reshape(1, D_MODEL), ln_b.reshape(1, D_MODEL))


def _permute_in_proj(w_in):
    pieces = [
        w_in[..., 0:512], w_in[..., 768:1280],
        w_in[..., 1280:2304], w_in[..., 2304:2816],
        w_in[..., 4352:4864],
        w_in[..., 4880:7952],
        w_in[..., 2816:4352],
        w_in[..., 512:640], w_in[..., 640:768],
        w_in[..., 4864:4880],
    ]
    pad = IN_WIDTH_PAD - sum(piece.shape[-1] for piece in pieces)
    pieces.append(jnp.zeros(w_in.shape[:-1] + (pad,), w_in.dtype))
    return jnp.concatenate(pieces, axis=-1).astype(BF16)


def kernel(x, c, ctx, c_ctx, w_ada, b_ada, w_in, a_sink, b_conv_w, b_conv_b, b_norm_g, b_norm_b,
           c_conv_w, c_a_log, c_dt_bias, c_norm_g, w_branch, w_out, ln_g, ln_b):
    B, n, _ = x.shape
    ctx_len = ctx.shape[1]
    assert ctx_len % ROW_TILE == 0 and n % ROW_TILE == 0 and n % GRID_W == 0

    xc = jnp.concatenate([ctx, x], axis=1)
    mod_rows = -(-(B + 1) // 8) * 8
    cc = jnp.concatenate([c, c_ctx[None, :], jnp.zeros((mod_rows - B - 1, D_MODEL), F32)], axis=0)
    mods_all = _ada_call(cc, w_ada, b_ada).reshape(DEPTH, mod_rows, 3, D_MODEL)
    w_perm = _permute_in_proj(w_in)
    wb16 = w_branch.astype(BF16)
    wo16 = w_out.astype(BF16)
    cos, sin = _rope_tables(n, ctx_len)

    for l in range(DEPTH):
        mods = mods_all[l]
        p, ab = _inproj_call(xc, mods, w_perm[l], ctx_len, BF16)
        ya = _attn_call(p, a_sink[l], cos, sin, ctx_len)
        yb = _conv_call(p, b_conv_w[l], b_conv_b[l], b_norm_g[l], b_norm_b[l], ctx_len)
        yc = _dn_call(p, ab, c_conv_w[l], c_a_log[l], c_dt_bias[l], c_norm_g[l], ctx_len)
        xc = _merge_call(ya, yb, yc, p, xc, mods, wb16[l], wo16[l], ln_g[l], ln_b[l], ctx_len,
                         latent_only=l == DEPTH - 1)
    return xc
```

```python
import functools

import jax
import jax.numpy as jnp
from jax import lax
from jax.experimental import pallas as pl
from jax.experimental.pallas import tpu as pltpu

F32 = jnp.float32
BF16 = jnp.bfloat16
HIGHEST = lax.Precision.HIGHEST

LANES = 128
VMEM_LIMIT = 56 * 1024 * 1024

D_MODEL = 1024
DEPTH = 4
GRID_W = 64
HEAD_DIM = 64
ATT_Q_HEADS = 8
ATT_KV_HEADS = 2
ATT_GROUP = ATT_Q_HEADS // ATT_KV_HEADS
ATT_WIDTH = 512
BLOCK = 128
ROPE_BASE = 10000.0
ROPE_PAIRS = HEAD_DIM // 4
CONV_WIDTH = 512
CONV_TAPS = 31
CONV_HALO = 16
DN_HEADS = 4
DN_HEAD_DIM = 128
DN_WIDTH = 512
CHUNK = 64
N_BRANCH = 3
DEEPNORM_ALPHA = (2 * DEPTH) ** 0.25
LN_EPS = 1e-5
RMS_EPS = 1e-6
NEG_INF = -1e30

OFF_AQ, OFF_AZ = 0, 512
OFF_BA, OFF_BB, OFF_BZ = 1024, 1536, 2048
OFF_CZ = 2560
OFF_GATE = 3072
OFF_CQ, OFF_CK, OFF_CV = 6144, 6656, 7168
OFF_AK, OFF_AV = 7680, 7808
OFF_AB = 7936
IN_WIDTH_PAD = 8192
IN_TILE_N = 1024
ROW_TILE = 256


def _silu(t):
    return t * jax.nn.sigmoid(t)


def _params(*sem):
    return pltpu.CompilerParams(dimension_semantics=sem, vmem_limit_bytes=VMEM_LIMIT)


def _ada_kernel(cc_ref, w_ref, b_ref, o_ref):
    s = _silu(cc_ref[...])
    o_ref[0] = jnp.dot(s, w_ref[0], precision=HIGHEST, preferred_element_type=F32) + b_ref[0]


def _ada_call(cc, w_ada, b_ada):
    rows = cc.shape[0]
    return pl.pallas_call(
        _ada_kernel,
        grid=(DEPTH, 3),
        in_specs=[
            pl.BlockSpec((rows, D_MODEL), lambda l, j: (0, 0)),
            pl.BlockSpec((1, D_MODEL, D_MODEL), lambda l, j: (l, 0, j)),
            pl.BlockSpec((1, 1, D_MODEL), lambda l, j: (l, 0, j)),
        ],
        out_specs=pl.BlockSpec((1, rows, D_MODEL), lambda l, j: (l, 0, j)),
        out_shape=jax.ShapeDtypeStruct((DEPTH, rows, 3 * D_MODEL), F32),
        compiler_params=_params("arbitrary", "arbitrary"),
        name="ada_mod",
    )(cc, w_ada, b_ada.reshape(DEPTH, 1, 3 * D_MODEL))


def _inproj_kernel(x_ref, mb_ref, mc_ref, w_ref, p_ref, ab_ref, u_scr, *, tm, ctx_len):
    i = pl.program_id(1)
    j = pl.program_id(2)

    @pl.when(j == 0)
    def _():
        row = lax.broadcasted_iota(jnp.int32, (tm, 1), 0) + i * tm
        is_ctx = row < ctx_len
        shift = jnp.where(is_ctx, mc_ref[0, 0:1, :], mb_ref[0, 0:1, :])
        scale = jnp.where(is_ctx, mc_ref[0, 1:2, :], mb_ref[0, 1:2, :])
        u_scr[...] = (x_ref[0] * (1.0 + scale) + shift).astype(BF16)

    w = w_ref[0, :, pl.ds(pl.multiple_of(j * IN_TILE_N, IN_TILE_N), IN_TILE_N)]
    p = jnp.dot(u_scr[...], w, preferred_element_type=F32)
    p_ref[0] = p.astype(p_ref.dtype)

    @pl.when(j == OFF_AB // IN_TILE_N)
    def _():
        ab_ref[0] = p[:, OFF_AB % IN_TILE_N:OFF_AB % IN_TILE_N + LANES]


def _inproj_call(xc, mods, w_all, layer, ctx_len, p_dtype):
    B, T, _ = xc.shape
    tm = T // 2
    nj = IN_WIDTH_PAD // IN_TILE_N
    ctx_row = B
    return pl.pallas_call(
        functools.partial(_inproj_kernel, tm=tm, ctx_len=ctx_len),
        grid=(B, T // tm, nj),
        in_specs=[
            pl.BlockSpec((1, tm, D_MODEL), lambda b, i, j: (b, i, 0)),
            pl.BlockSpec((1, 3, D_MODEL), lambda b, i, j: (b, 0, 0)),
            pl.BlockSpec((1, 3, D_MODEL), lambda b, i, j: (ctx_row, 0, 0)),
            pl.BlockSpec((1, D_MODEL, IN_WIDTH_PAD), lambda b, i, j: (layer, 0, 0)),
        ],
        out_specs=[
            pl.BlockSpec((1, tm, IN_TILE_N), lambda b, i, j: (b, i, j)),
            pl.BlockSpec((1, tm, LANES), lambda b, i, j: (b, i, 0)),
        ],
        out_shape=[
            jax.ShapeDtypeStruct((B, T, IN_WIDTH_PAD), p_dtype),
            jax.ShapeDtypeStruct((B, T, LANES), F32),
        ],
        scratch_shapes=[pltpu.VMEM((tm, D_MODEL), BF16)],
        compiler_params=_params("arbitrary", "arbitrary", "arbitrary"),
        name="in_proj",
    )(xc, mods, mods, w_all)


def _rope(t, cos, sin_signed):
    lane = lax.broadcasted_iota(jnp.int32, (1, LANES), 1)
    partner = jnp.where((lane % 32) < 16, pltpu.roll(t, LANES - 16, 1), pltpu.roll(t, 16, 1))
    return t * cos + partner * sin_signed


def _attn_kernel(sink_ref, q_ref, z_ref, k_ref, v_ref, cq_ref, sq_ref, ck_ref, sk_ref, o_ref, kr_scr,
                 *, ctx_len, n_blocks):
    i = pl.program_id(1)
    ctx_blocks = ctx_len // BLOCK

    @pl.when(i == 0)
    def _():
        kr_scr[...] = _rope(k_ref[0].astype(F32), ck_ref[...], sk_ref[...]).astype(BF16)

    q = q_ref[0].astype(F32) * (HEAD_DIM ** -0.5)
    cq = cq_ref[...]
    sq = sq_ref[...]
    q_pairs = [_rope(q[:, LANES * t:LANES * (t + 1)], cq, sq) for t in range(ATT_WIDTH // LANES)]

    def q_head(hq):
        off = (hq % 2) * HEAD_DIM
        return q_pairs[hq // 2][:, off:off + HEAD_DIM]

    left = pl.multiple_of(jnp.maximum(i - 1, 0) * BLOCK, BLOCK)
    mid = pl.multiple_of(i * BLOCK, BLOCK)
    right = pl.multiple_of(jnp.minimum(i + 1, n_blocks - 1) * BLOCK, BLOCK)
    kcat = jnp.concatenate([kr_scr[0:ctx_len, :], kr_scr[pl.ds(left, BLOCK), :],
                            kr_scr[pl.ds(mid, BLOCK), :], kr_scr[pl.ds(right, BLOCK), :]], axis=0)
    vcat = jnp.concatenate([v_ref[0, 0:ctx_len, :], v_ref[0, pl.ds(left, BLOCK), :],
                            v_ref[0, pl.ds(mid, BLOCK), :], v_ref[0, pl.ds(right, BLOCK), :]],
                           axis=0).astype(BF16)

    rows = ATT_GROUP * BLOCK
    is_lat = i >= ctx_blocks
    has_left = i >= ctx_blocks + 1
    has_right = jnp.logical_and(is_lat, i <= n_blocks - 2)
    rq = lax.broadcasted_iota(jnp.int32, (rows, BLOCK), 0) % BLOCK
    kc = lax.broadcasted_iota(jnp.int32, (rows, BLOCK), 1)
    grp = lax.broadcasted_iota(jnp.int32, (rows, 1), 0) // BLOCK
    neg_left = jnp.where(has_left, 0.0, NEG_INF)
    neg_mid = jnp.where(is_lat, 0.0, NEG_INF)
    neg_right = jnp.where(has_right, 0.0, NEG_INF)
    bias_left = jnp.where(kc >= rq, neg_left, NEG_INF)
    bias_right = jnp.where(kc <= rq, neg_right, NEG_INF)

    def masked(s):
        band = s[:, ctx_len:]
        return jnp.concatenate([s[:, :ctx_len], band[:, :BLOCK] + bias_left, band[:, BLOCK:2 * BLOCK] + neg_mid,
                                band[:, 2 * BLOCK:] + bias_right], axis=1)

    heads = range(ATT_KV_HEADS)
    qh = [jnp.concatenate([q_head(h * ATT_GROUP + g) for g in range(ATT_GROUP)], axis=0).astype(BF16) for h in heads]
    s = [masked(lax.dot_general(qh[h], kcat[:, h * HEAD_DIM:(h + 1) * HEAD_DIM], (((1,), (1,)), ((), ())),
                                preferred_element_type=F32)) for h in heads]
    sk = []
    for h in heads:
        col = jnp.zeros((rows, 1), F32)
        for g in range(ATT_GROUP):
            col = jnp.where(grp == g, sink_ref[h * ATT_GROUP + g], col)
        sk.append(col)
    m = [jnp.maximum(jnp.max(s[h], axis=1, keepdims=True), sk[h]) for h in heads]
    p = [jnp.exp(s[h] - m[h]) for h in heads]
    den = [jnp.sum(p[h], axis=1, keepdims=True) + jnp.exp(sk[h] - m[h]) for h in heads]
    o = [jnp.dot(p[h].astype(BF16), vcat[:, h * HEAD_DIM:(h + 1) * HEAD_DIM], preferred_element_type=F32) / den[h]
         for h in heads]
    att = jnp.concatenate([o[h][g * BLOCK:(g + 1) * BLOCK] for h in heads for g in range(ATT_GROUP)], axis=1)
    o_ref[0] = (att * _silu(z_ref[0].astype(F32))).astype(o_ref.dtype)


def _attn_call(p, sink, cos, sin, ctx_len):
    B, T, _ = p.shape
    nb = T // BLOCK
    return pl.pallas_call(
        functools.partial(_attn_kernel, ctx_len=ctx_len, n_blocks=nb),
        grid=(B, nb),
        in_specs=[
            pl.BlockSpec(memory_space=pltpu.SMEM),
            pl.BlockSpec((1, BLOCK, ATT_WIDTH), lambda b, i: (b, i, OFF_AQ // ATT_WIDTH)),
            pl.BlockSpec((1, BLOCK, ATT_WIDTH), lambda b, i: (b, i, OFF_AZ // ATT_WIDTH)),
            pl.BlockSpec((1, T, LANES), lambda b, i: (b, 0, OFF_AK // LANES)),
            pl.BlockSpec((1, T, LANES), lambda b, i: (b, 0, OFF_AV // LANES)),
            pl.BlockSpec((BLOCK, LANES), lambda b, i: (i, 0)),
            pl.BlockSpec((BLOCK, LANES), lambda b, i: (i, 0)),
            pl.BlockSpec((T, LANES), lambda b, i: (0, 0)),
            pl.BlockSpec((T, LANES), lambda b, i: (0, 0)),
        ],
        out_specs=pl.BlockSpec((1, BLOCK, ATT_WIDTH), lambda b, i: (b, i, 0)),
        out_shape=jax.ShapeDtypeStruct((B, T, ATT_WIDTH), BF16),
        scratch_shapes=[pltpu.VMEM((T, LANES), BF16)],
        compiler_params=_params("arbitrary", "arbitrary"),
        name="attention",
    )(sink, p, p, p, p, cos, sin, cos, sin)


def _rope_tables(n, ctx_len):
    pos = jnp.arange(n)
    row = (pos // GRID_W).astype(F32)
    colp = (pos % GRID_W).astype(F32)
    inv = ROPE_BASE ** (-jnp.arange(ROPE_PAIRS, dtype=F32) / ROPE_PAIRS)
    lane = jnp.arange(LANES)
    axis = (lane % HEAD_DIM) // (2 * ROPE_PAIRS)
    pair = lane % ROPE_PAIRS
    ang = jnp.where(axis[None, :] == 0, row[:, None], colp[:, None]) * inv[pair][None, :]
    sign = jnp.where((lane % (2 * ROPE_PAIRS)) < ROPE_PAIRS, -1.0, 1.0).astype(F32)
    cos = jnp.concatenate([jnp.ones((ctx_len, LANES), F32), jnp.cos(ang)], axis=0)
    sin = jnp.concatenate([jnp.zeros((ctx_len, LANES), F32), jnp.sin(ang) * sign[None, :]], axis=0)
    return cos, sin


def _conv_kernel(a_ref, b_ref, z_ref, ap_ref, bp_ref, an_ref, bn_ref, w_ref, cb_ref, g_ref, nb_ref,
                 o_ref, h_scr, shift_scr, *, ctx_tiles, n_tiles):
    i = pl.program_id(1)

    def glu(a, b):
        return a.astype(F32) * jax.nn.sigmoid(b.astype(F32))

    has_prev = jnp.logical_and(i != 0, i != ctx_tiles)
    has_next = jnp.logical_and(i != ctx_tiles - 1, i != n_tiles - 1)
    h_scr[0:CONV_HALO, :] = jnp.where(has_prev, glu(ap_ref[0], bp_ref[0]), 0.0)
    h_scr[CONV_HALO:CONV_HALO + ROW_TILE, :] = glu(a_ref[0], b_ref[0])
    h_scr[CONV_HALO + ROW_TILE:, :] = jnp.where(has_next, glu(an_ref[0], bn_ref[0]), 0.0)

    base = CONV_HALO - CONV_TAPS // 2
    span = ROW_TILE + CONV_HALO + 8
    acc = jnp.zeros((ROW_TILE, CONV_WIDTH), F32)
    for r in range(8):
        if r:
            shift_scr[...] = h_scr[r:r + span, :]
        src = shift_scr if r else h_scr
        for off in range(r, base + CONV_TAPS, 8):
            k = off - base
            if k >= 0:
                acc = acc + w_ref[k:k + 1, :] * src[off - r:off - r + ROW_TILE, :]
    acc = acc + cb_ref[...]
    mu = jnp.mean(acc, axis=1, keepdims=True)
    cen = acc - mu
    var = jnp.mean(cen * cen, axis=1, keepdims=True)
    hn = cen * lax.rsqrt(var + LN_EPS) * g_ref[...] + nb_ref[...]
    o_ref[0] = (_silu(hn) * _silu(z_ref[0].astype(F32))).astype(o_ref.dtype)


def _conv_call(p, conv_w, conv_b, norm_g, norm_b, ctx_len):
    B, T, _ = p.shape
    nt = T // ROW_TILE
    hpt = ROW_TILE // CONV_HALO
    last_halo = T // CONV_HALO - 1
    ca, cb_, cz = OFF_BA // CONV_WIDTH, OFF_BB // CONV_WIDTH, OFF_BZ // CONV_WIDTH

    def prev_map(col):
        return lambda b, i: (b, jnp.maximum(i * hpt - 1, 0), col)

    def next_map(col):
        return lambda b, i: (b, jnp.minimum((i + 1) * hpt, last_halo), col)

    row = lambda v: v.reshape(1, CONV_WIDTH)
    vec_spec = pl.BlockSpec((1, CONV_WIDTH), lambda b, i: (0, 0))
    return pl.pallas_call(
        functools.partial(_conv_kernel, ctx_tiles=ctx_len // ROW_TILE, n_tiles=nt),
        grid=(B, nt),
        in_specs=[
            pl.BlockSpec((1, ROW_TILE, CONV_WIDTH), lambda b, i: (b, i, ca)),
            pl.BlockSpec((1, ROW_TILE, CONV_WIDTH), lambda b, i: (b, i, cb_)),
            pl.BlockSpec((1, ROW_TILE, CONV_WIDTH), lambda b, i: (b, i, cz)),
            pl.BlockSpec((1, CONV_HALO, CONV_WIDTH), prev_map(ca)),
            pl.BlockSpec((1, CONV_HALO, CONV_WIDTH), prev_map(cb_)),
            pl.BlockSpec((1, CONV_HALO, CONV_WIDTH), next_map(ca)),
            pl.BlockSpec((1, CONV_HALO, CONV_WIDTH), next_map(cb_)),
            pl.BlockSpec((CONV_TAPS, CONV_WIDTH), lambda b, i: (0, 0)),
            vec_spec, vec_spec, vec_spec,
        ],
        out_specs=pl.BlockSpec((1, ROW_TILE, CONV_WIDTH), lambda b, i: (b, i, 0)),
        out_shape=jax.ShapeDtypeStruct((B, T, CONV_WIDTH), BF16),
        scratch_shapes=[pltpu.VMEM((ROW_TILE + 2 * CONV_HALO, CONV_WIDTH), F32),
                        pltpu.VMEM((ROW_TILE + CONV_HALO + 8, CONV_WIDTH), F32)],
        compiler_params=_params("arbitrary", "arbitrary"),
        name="conformer_conv",
    )(p, p, p, p, p, p, p, conv_w, row(conv_b), row(norm_g), row(norm_b))


DN_PACK = 4
DN_GROUP = 3
DN_HEADS_PER_STEP = 2


def _split_bf16(t):
    hi = t.astype(BF16)
    return hi, (t - hi.astype(F32)).astype(BF16)


def _dot(a, b):
    return jnp.dot(a, b, preferred_element_type=F32)


def _dn_kernel(q_ref, k_ref, v_ref, z_ref, ab_ref, wq_ref, wk_ref, wv_ref, alog_ref, dtb_ref, gain_ref,
               o_ref, qs, ks, vs, kts, gb, bb, grow, o_scr, *, ctx_len, total):
    head0 = pl.program_id(1) * DN_HEADS_PER_STEP
    n_pairs = total // (2 * CHUNK)
    ctx_pairs = ctx_len // (2 * CHUNK)
    n_stages = n_pairs // DN_GROUP
    tr = ROW_TILE
    wide = DN_PACK * CHUNK
    heads = range(DN_HEADS_PER_STEP)
    lane = lax.broadcasted_iota(jnp.int32, (1, LANES), 1)
    rowi = lax.broadcasted_iota(jnp.int32, (tr, 1), 0)

    def short_conv(ref, w_ref, r0, hh):
        cols = slice(hh * LANES, (hh + 1) * LANES)
        x = ref[0, r0:r0 + tr, cols].astype(F32)
        zero = jnp.zeros((1, LANES), F32)
        prev_row = zero if r0 in (0, ctx_len) else ref[0, r0 - 16:r0, cols].astype(F32)[15:16]
        next_row = zero if r0 + tr in (ctx_len, total) else ref[0, r0 + tr:r0 + tr + 16, cols].astype(F32)[0:1]
        xp = jnp.where(rowi == 0, prev_row, pltpu.roll(x, 1, 0))
        xn = jnp.where(rowi == tr - 1, next_row, pltpu.roll(x, tr - 1, 0))
        y = w_ref[0:1, cols] * xp + w_ref[1:2, cols] * x + w_ref[2:3, cols] * xn
        return _silu(y)

    def l2norm(t):
        return t * lax.rsqrt(jnp.sum(t * t, axis=1, keepdims=True) + RMS_EPS)

    def lane_pick(t, idx):
        return jnp.broadcast_to(jnp.sum(jnp.where(lane == idx, t, 0.0), axis=1, keepdims=True), t.shape)

    for r0 in range(0, total, tr):
        a = ab_ref[0, r0:r0 + tr, :]
        pre = a + dtb_ref[...]
        softplus = jnp.maximum(pre, 0.0) + jnp.log1p(jnp.exp(-jnp.abs(pre)))
        g_all = -jnp.exp(alog_ref[...]) * softplus
        b_all = jax.nn.sigmoid(a)
        for hh in heads:
            qs[hh, r0:r0 + tr, :] = l2norm(short_conv(q_ref, wq_ref, r0, hh)) * (DN_HEAD_DIM ** -0.5)
            kn = l2norm(short_conv(k_ref, wk_ref, r0, hh))
            ks[hh, r0:r0 + tr, :] = kn
            kts[hh, :, r0:r0 + tr] = kn.T
            vs[hh, r0:r0 + tr, :] = short_conv(v_ref, wv_ref, r0, hh)
            for d in range(2):
                g_sel = lane_pick(g_all, d * DN_HEADS + head0 + hh)
                gb[hh, d, r0:r0 + tr, :] = g_sel
                grow[hh, d, :, r0:r0 + tr] = g_sel.T[0:8, :]
                bb[hh, d, r0:r0 + tr, :] = lane_pick(b_all, 2 * DN_HEADS + d * DN_HEADS + head0 + hh)

    prow = lax.broadcasted_iota(jnp.int32, (CHUNK, wide), 0)
    plane = lax.broadcasted_iota(jnp.int32, (CHUNK, wide), 1)
    pcol = plane % CHUNK
    ahead = jnp.where(plane < 2 * CHUNK, prow - pcol, pcol - prow)
    incl = ahead >= 0
    strict = ahead > 0
    eye = jnp.where(prow == pcol, 1.0, 0.0).astype(F32)
    level_masks = tuple(jnp.logical_and(prow // (2 * s) == pcol // (2 * s), prow // s != pcol // s)
                        for s in (1, 2, 4, 8, 16, 32))
    slot_masks = tuple(jnp.where(plane // CHUNK == i, 1.0, 0.0).astype(BF16) for i in range(DN_PACK))
    ci = lax.broadcasted_iota(jnp.int32, (CHUNK, CHUNK), 0)
    cj = lax.broadcasted_iota(jnp.int32, (CHUNK, CHUNK), 1)
    lower_ones = jnp.where(cj <= ci, 1.0, 0.0).astype(BF16)
    wr = lax.broadcasted_iota(jnp.int32, (wide, wide), 0)
    wc = lax.broadcasted_iota(jnp.int32, (wide, wide), 1)
    same_slot = wr // CHUNK == wc // CHUNK
    along = jnp.where(wr < 2 * CHUNK, wc - wr, wr - wc) >= 0
    row_cumsum = jnp.where(jnp.logical_and(same_slot, along), 1.0, 0.0).astype(BF16)
    left_half = lax.broadcasted_iota(jnp.int32, (CHUNK, LANES), 1) < CHUNK
    left_half_row = lane < CHUNK
    zeros_tile = jnp.zeros((CHUNK, LANES), F32)

    def block_diag(y):
        return jnp.concatenate([y * m for m in slot_masks], axis=0)

    def pack(t0, t1, t2, t3, mask=left_half):
        return jnp.concatenate([jnp.where(mask, t0, t1), jnp.where(mask, t2, t3)], axis=1)

    def scan_pairs(step):
        return step, jnp.where(step < ctx_pairs, ctx_pairs - 1 - step, n_pairs - 1 + ctx_pairs - step)

    def chunk_rows(chunk):
        return pl.ds(pl.multiple_of(chunk * CHUNK, CHUNK), CHUNK)

    def pair_cols(pair):
        return pl.ds(pl.multiple_of(pair * LANES, LANES), LANES)

    def local_stages(stage_idx, out):
        units = []
        for hh in heads:
            for j in range(DN_GROUP):
                pairs = scan_pairs(stage_idx * DN_GROUP + j)
                rows = [chunk_rows(2 * pairs[d] + i) for d in range(2) for i in range(2)]
                k = [ks[hh, r, :] for r in rows]
                q = [qs[hh, r, :] for r in rows]
                v = [vs[hh, r, :] for r in rows]
                g = [gb[hh, i // 2, rows[i], :] for i in range(DN_PACK)]
                be = [bb[hh, i // 2, rows[i], :] for i in range(DN_PACK)]
                g_hi, g_lo = _split_bf16(jnp.concatenate(g, axis=1))
                prefix = _dot(lower_ones, g_hi) + _dot(lower_ones, g_lo)
                gcol, g_last = [], []
                for i in range(DN_PACK):
                    pf = prefix[:, i * LANES:(i + 1) * LANES]
                    tot = pf[CHUNK - 1:CHUNK, :]
                    gcol.append(pf if i < 2 else tot - pf + g[i])
                    g_last.append(tot)
                r_hi, r_lo = _split_bf16(jnp.concatenate(
                    [jnp.broadcast_to(grow[hh, d, 0:1, pair_cols(pairs[d])], (8, LANES)) for d in range(2)], axis=1))
                grow_cum = (_dot(r_hi, row_cumsum) + _dot(r_lo, row_cumsum))[0:1, :]
                dm = jnp.exp(jnp.where(incl, pack(*gcol) - grow_cum, NEG_INF))
                kkqk = []
                for d in range(2):
                    k0, k1, q0, q1 = k[2 * d], k[2 * d + 1], q[2 * d], q[2 * d + 1]
                    lhs = jnp.concatenate([jnp.concatenate([k0, q0], axis=0),
                                           jnp.concatenate([k1, q1], axis=0)], axis=1).astype(BF16)
                    rhs = jnp.concatenate([jnp.concatenate([k0, zeros_tile], axis=1),
                                           jnp.concatenate([zeros_tile, k1], axis=1)], axis=0).astype(BF16)
                    kkqk.append(lax.dot_general(lhs, rhs, (((1,), (1,)), ((), ())), preferred_element_type=F32))
                kk = jnp.concatenate([t[:CHUNK] for t in kkqk], axis=1)
                qk = jnp.concatenate([t[CHUNK:] for t in kkqk], axis=1)
                a_mat = jnp.where(strict, kk * dm, 0.0) * pack(*be)
                units.append(dict(
                    hh=hh, pairs=pairs, rows=rows, gcol=gcol, g_last=g_last, grow_cum=grow_cum,
                    aqk=jnp.where(incl, qk * dm, 0.0),
                    a_bd=block_diag(a_mat.astype(BF16)),
                    inv=eye - jnp.where(level_masks[0], a_mat, 0.0)))
        yield

        for m in level_masks[1:]:
            inv_b = [un["inv"].astype(BF16) for un in units]
            t1 = [_dot(ib, un["a_bd"]) for ib, un in zip(inv_b, units)]
            yield
            t2 = [_dot(t.astype(BF16), block_diag(ib)) for t, ib in zip(t1, inv_b)]
            yield
            for un, t in zip(units, t2):
                un["inv"] = un["inv"] - jnp.where(m, t, 0.0)

        for un in units:
            hh, rows, gcol, g_last = un["hh"], un["rows"], un["gcol"], un["g_last"]
            k = [ks[hh, r, :] for r in rows]
            q = [qs[hh, r, :] for r in rows]
            v = [vs[hh, r, :] for r in rows]
            be = [bb[hh, i // 2, rows[i], :] for i in range(DN_PACK)]
            ek = [jnp.exp(t) for t in gcol]
            rstack = jnp.concatenate(
                [jnp.concatenate([v[i] * be[i], k[i] * be[i] * ek[i]], axis=1) for i in range(DN_PACK)],
                axis=0).astype(BF16)
            uw = _dot(block_diag(un["inv"].astype(BF16)), rstack)
            e_row = jnp.exp(pack(*g_last, mask=left_half_row) - un["grow_cum"])
            out.append(dict(
                u=[uw[i * CHUNK:(i + 1) * CHUNK, :LANES] for i in range(DN_PACK)],
                wq=[jnp.concatenate([uw[i * CHUNK:(i + 1) * CHUNK, LANES:], q[i] * ek[i]], axis=0).astype(BF16)
                    for i in range(DN_PACK)],
                ka=[jnp.concatenate([kts[un["hh"], :, pair_cols(un["pairs"][d])] * e_row[:, d * LANES:(d + 1) * LANES],
                                     un["aqk"][:, d * LANES:(d + 1) * LANES]], axis=0).astype(BF16)
                    for d in range(2)],
                cd=[jnp.broadcast_to(jnp.exp(t), (8, LANES)) for t in g_last]))
        yield

    def scan_stages(stage_idx, res, states):
        for j in range(DN_GROUP):
            pairs = scan_pairs(stage_idx * DN_GROUP + j)
            for sub in range(2):
                chains = [(hh, d, sub if d == 0 else 1 - sub) for hh in heads for d in range(2)]
                units = [res[hh * DN_GROUP + j] for hh, _, _ in chains]
                ws = [_dot(un["wq"][2 * d + half], states[hh][d].astype(BF16))
                      for un, (hh, d, half) in zip(units, chains)]
                yield
                v_new = [un["u"][2 * d + half] - w[:CHUNK] for un, w, (_, d, half) in zip(units, ws, chains)]
                v_rows = [jnp.concatenate([vn, zeros_tile] if half == 0 else [zeros_tile, vn], axis=0).astype(BF16)
                          for vn, (_, _, half) in zip(v_new, chains)]
                ka = [_dot(un["ka"][d], vr) for un, vr, (_, d, _) in zip(units, v_rows, chains)]
                yield
                for un, w, kav, (hh, d, half) in zip(units, ws, ka, chains):
                    o_scr[hh, d, chunk_rows(2 * pairs[d] + half), :] = w[CHUNK:] + kav[LANES:]
                    states[hh][d] = states[hh][d] * un["cd"][2 * d + half][0:1, :] + kav[:LANES]

    def run(*gens):
        live = list(gens)
        while live:
            for gen in list(live):
                if next(gen, StopIteration) is StopIteration:
                    live.remove(gen)

    zero_state = jnp.zeros((DN_HEAD_DIM, DN_HEAD_DIM), F32)
    first = []
    run(local_stages(0, first))

    def pipeline(stage_idx, carry):
        res, states = carry
        states = [list(st) for st in states]
        new = []
        run(local_stages(stage_idx, new), scan_stages(stage_idx - 1, res, states))
        return new, states

    res, states = lax.fori_loop(1, n_stages, pipeline, (first, [[zero_state, zero_state] for _ in heads]))
    run(scan_stages(n_stages - 1, res, [list(st) for st in states]))

    for hh in heads:
        cols = slice(hh * LANES, (hh + 1) * LANES)
        for r0 in range(0, total, tr):
            o = o_scr[hh, 0, r0:r0 + tr, :] + o_scr[hh, 1, r0:r0 + tr, :]
            o = o * lax.rsqrt(jnp.mean(o * o, axis=1, keepdims=True) + RMS_EPS) * gain_ref[...]
            o_ref[0, r0:r0 + tr, cols] = (o * _silu(z_ref[0, r0:r0 + tr, cols].astype(F32))).astype(o_ref.dtype)


def _dn_call(p, ab, conv_w, a_log, dt_bias, gain, ctx_len):
    B, T, _ = p.shape
    hp = DN_HEADS_PER_STEP
    width = hp * LANES
    assert T % (2 * CHUNK * DN_GROUP) == 0 and ctx_len % (2 * CHUNK) == 0 and DN_HEADS % hp == 0

    def pad_row(v):
        return jnp.zeros((1, LANES), F32).at[0, :v.size].set(v.reshape(-1))

    def col(off):
        return lambda b, h: (b, 0, off // width + h)

    seq = lambda off: pl.BlockSpec((1, T, width), col(off))
    tap = lambda base: pl.BlockSpec((3, width), lambda b, h: (0, base * DN_HEADS // hp + h))
    vec = pl.BlockSpec((1, LANES), lambda b, h: (0, 0))
    return pl.pallas_call(
        functools.partial(_dn_kernel, ctx_len=ctx_len, total=T),
        grid=(B, DN_HEADS // hp),
        in_specs=[
            seq(OFF_CQ), seq(OFF_CK), seq(OFF_CV), seq(OFF_CZ),
            pl.BlockSpec((1, T, LANES), lambda b, h: (b, 0, 0)),
            tap(0), tap(1), tap(2),
            vec, vec, vec,
        ],
        out_specs=pl.BlockSpec((1, T, width), lambda b, h: (b, 0, h)),
        out_shape=jax.ShapeDtypeStruct((B, T, DN_WIDTH), BF16),
        scratch_shapes=[
            pltpu.VMEM((hp, T, LANES), F32), pltpu.VMEM((hp, T, LANES), F32), pltpu.VMEM((hp, T, LANES), F32),
            pltpu.VMEM((hp, LANES, T), F32),
            pltpu.VMEM((hp, 2, T, LANES), F32), pltpu.VMEM((hp, 2, T, LANES), F32),
            pltpu.VMEM((hp, 2, 8, T), F32),
            pltpu.VMEM((hp, 2, T, LANES), F32),
        ],
        compiler_params=_params("arbitrary", "arbitrary"),
        name="gated_deltanet",
    )(p, p, p, p, ab, conv_w, conv_w, conv_w, pad_row(a_log), pad_row(dt_bias), gain.reshape(1, LANES))


def _merge_kernel(ya_ref, yb_ref, yc_ref, g0_ref, g1_ref, g2_ref, x_ref, mb_ref, mc_ref, wb_ref, wo_ref,
                  lg_ref, lb_ref, o_ref, *, ctx_tiles, first_tile):
    i = pl.program_id(1) + first_tile

    def branch(y_ref, g_ref, r):
        proj = jnp.dot(y_ref[0], wb_ref[0, r], preferred_element_type=F32)
        return jax.nn.sigmoid(g_ref[0].astype(F32)) * proj

    merged = branch(ya_ref, g0_ref, 0) + branch(yb_ref, g1_ref, 1) + branch(yc_ref, g2_ref, 2)
    out = jnp.dot(merged.astype(BF16), wo_ref[0], preferred_element_type=F32)
    gate = jnp.where(i < ctx_tiles, mc_ref[0, 2:3, :], mb_ref[0, 2:3, :])
    t = DEEPNORM_ALPHA * x_ref[0] + gate * out
    mu = jnp.mean(t, axis=1, keepdims=True)
    cen = t - mu
    var = jnp.mean(cen * cen, axis=1, keepdims=True)
    o_ref[0] = cen * lax.rsqrt(var + LN_EPS) * lg_ref[...] + lb_ref[...]


def _merge_call(ya, yb, yc, p, xc, mods, w_branch, w_out, layer, ln_g, ln_b, ctx_len, latent_only):
    B, T, _ = xc.shape
    ctx_row = B
    gcol = OFF_GATE // D_MODEL
    ctx_tiles = ctx_len // ROW_TILE
    first = ctx_tiles if latent_only else 0
    n_tiles = T // ROW_TILE - first
    y_spec = pl.BlockSpec((1, ROW_TILE, ATT_WIDTH), lambda b, i: (b, i + first, 0))
    gate_spec = lambda r: pl.BlockSpec((1, ROW_TILE, D_MODEL), lambda b, i: (b, i + first, gcol + r))
    vec = pl.BlockSpec((1, D_MODEL), lambda b, i: (0, 0))
    return pl.pallas_call(
        functools.partial(_merge_kernel, ctx_tiles=ctx_tiles, first_tile=first),
        grid=(B, n_tiles),
        in_specs=[
            y_spec, y_spec, y_spec,
            gate_spec(0), gate_spec(1), gate_spec(2),
            pl.BlockSpec((1, ROW_TILE, D_MODEL), lambda b, i: (b, i + first, 0)),
            pl.BlockSpec((1, 3, D_MODEL), lambda b, i: (b, 0, 0)),
            pl.BlockSpec((1, 3, D_MODEL), lambda b, i: (ctx_row, 0, 0)),
            pl.BlockSpec((1, N_BRANCH, ATT_WIDTH, D_MODEL), lambda b, i: (layer, 0, 0, 0)),
            pl.BlockSpec((1, D_MODEL, D_MODEL), lambda b, i: (layer, 0, 0)),
            vec, vec,
        ],
        out_specs=pl.BlockSpec((1, ROW_TILE, D_MODEL), lambda b, i: (b, i, 0)),
        out_shape=jax.ShapeDtypeStruct((B, n_tiles * ROW_TILE, D_MODEL), F32),
        compiler_params=_params("arbitrary", "arbitrary"),
        name="merge",
    )(ya, yb, yc, p, p, p, xc, mods, mods, w_branch, w_out, ln_g.reshape(1, D_MODEL), ln_b.reshape(1, D_MODEL))


def _permute_in_proj(w_in):
    pieces = [
        w_in[..., 0:512], w_in[..., 768:1280],
        w_in[..., 1280:2304], w_in[..., 2304:2816],
        w_in[..., 4352:4864],
        w_in[..., 4880:7952],
        w_in[..., 2816:4352],
        w_in[..., 512:640], w_in[..., 640:768],
        w_in[..., 4864:4880],
    ]
    pad = IN_WIDTH_PAD - sum(piece.shape[-1] for piece in pieces)
    pieces.append(jnp.zeros(w_in.shape[:-1] + (pad,), w_in.dtype))
    return jnp.concatenate(pieces, axis=-1).astype(BF16)


def kernel(x, c, ctx, c_ctx, w_ada, b_ada, w_in, a_sink, b_conv_w, b_conv_b, b_norm_g, b_norm_b,
           c_conv_w, c_a_log, c_dt_bias, c_norm_g, w_branch, w_out, ln_g, ln_b):
    B, n, _ = x.shape
    ctx_len = ctx.shape[1]
    assert ctx_len % ROW_TILE == 0 and n % ROW_TILE == 0 and n % GRID_W == 0

    xc = jnp.concatenate([ctx, x], axis=1)
    mod_rows = -(-(B + 1) // 8) * 8
    cc = jnp.concatenate([c, c_ctx[None, :], jnp.zeros((mod_rows - B - 1, D_MODEL), F32)], axis=0)
    mods_all = _ada_call(cc, w_ada, b_ada).reshape(DEPTH, mod_rows, 3, D_MODEL)
    w_perm = _permute_in_proj(w_in)
    wb16 = w_branch.astype(BF16)
    wo16 = w_out.astype(BF16)
    cos, sin = _rope_tables(n, ctx_len)

    for l in range(DEPTH):
        mods = mods_all[l]
        p, ab = _inproj_call(xc, mods, w_perm, l, ctx_len, BF16)
        ya = _attn_call(p, a_sink[l], cos, sin, ctx_len)
        yb = _conv_call(p, b_conv_w[l], b_conv_b[l], b_norm_g[l], b_norm_b[l], ctx_len)
        yc = _dn_call(p, ab, c_conv_w[l], c_a_log[l], c_dt_bias[l], c_norm_g[l], ctx_len)
        xc = _merge_call(ya, yb, yc, p, xc, mods, wb16, wo16, l, ln_g[l], ln_b[l], ctx_len,
                         latent_only=l == DEPTH - 1)
    return xc
```
